```python
import math
import jax, jax.numpy as jnp
from jax import lax
import numpy as np

D_MODEL = 1024
BATCH = 8
SEQ = 8192
DEPTH = 1

ATTN_WIDTH = 512
N_HEADS = 8
HEAD_DIM = 64
N_KV_GROUPS = 2
HEADS_PER_GROUP = N_HEADS // N_KV_GROUPS
KV_WIDTH = N_KV_GROUPS * HEAD_DIM
CMP_LEN = 32
CMP_STRIDE = 16
CMP_HIDDEN = 256
SEL_BLOCK = 64
SEL_TOPN = 16
WINDOW = 512
Q_BLOCK = 64
N_BRANCH = 3
ROPE_THETA = 10000.0
SSM_WIDTH = D_MODEL - ATTN_WIDTH
SSM_GROUP = 16
SSM_GROUPS = SSM_WIDTH // SSM_GROUP
SSM_STATE = 64
SSM_CHUNK = 128
DT_MIN = 1e-3
DT_MAX = 1e-1
EPS = 1e-6
IN_WIDTH = ATTN_WIDTH + 6 * KV_WIDTH + ATTN_WIDTH + N_BRANCH * N_HEADS + 2 * SSM_WIDTH

kernel_name = "hymba_nsa_s5_hybrid"


def rms_norm(x, g):
    xf = x.astype(jnp.float32)
    y = xf * lax.rsqrt(jnp.mean(xf * xf, axis=-1, keepdims=True) + EPS)
    return (y * g.astype(jnp.float32)).astype(x.dtype)


def rope_tables(T, dtype):
    inv = 1.0 / (ROPE_THETA ** (jnp.arange(0, HEAD_DIM, 2, dtype=jnp.float32) / HEAD_DIM))
    ang = jnp.arange(T, dtype=jnp.float32)[:, None] * inv[None, :]
    return jnp.cos(ang).astype(dtype), jnp.sin(ang).astype(dtype)


def apply_rope(x, cos, sin):
    x1, x2 = jnp.split(x, 2, axis=-1)
    c = cos[None, :, None, :]
    s = sin[None, :, None, :]
    return jnp.concatenate([x1 * c - x2 * s, x2 * c + x1 * s], axis=-1)


def masked_softmax(s, mask):
    s = jnp.where(mask, s.astype(jnp.float32), -jnp.inf)
    m = jnp.max(s, axis=-1, keepdims=True)
    m = jnp.where(jnp.isfinite(m), m, 0.0)
    e = jnp.where(mask, jnp.exp(s - m), 0.0)
    return e / jnp.maximum(jnp.sum(e, axis=-1, keepdims=True), 1e-30)


def compress(kv_raw, pos, w1, w2):
    B, T = kv_raw.shape[0], kv_raw.shape[1]
    nc = (T - CMP_LEN) // CMP_STRIDE + 1
    idx = jnp.arange(nc)[:, None] * CMP_STRIDE + jnp.arange(CMP_LEN)[None, :]
    blocks = kv_raw[:, idx] + pos[None, None, :, None, :]
    blocks = blocks.transpose(0, 1, 3, 2, 4).reshape(B, nc, N_KV_GROUPS, CMP_LEN * HEAD_DIM)
    return jax.nn.gelu(blocks @ w1) @ w2


def block_importance(p_cmp, nsb):
    r = SEL_BLOCK // CMP_STRIDE
    ov = CMP_LEN // CMP_STRIDE
    nc = p_cmp.shape[-1]
    left = ov - 1
    right = r * nsb + r - nc
    pp = jnp.pad(p_cmp, [(0, 0)] * (p_cmp.ndim - 1) + [(left, right)])
    terms = []
    for m in range(r):
        for n in range(ov):
            start = m - n + left
            terms.append(pp[..., start:start + r * (nsb - 1) + 1:r])
    return sum(terms[1:], terms[0])


def nsa_attention(q, kc, vc, k_slc, v_slc, k_win, v_win, gates):
    B, T = q.shape[0], q.shape[1]
    cos, sin = rope_tables(T, q.dtype)
    q_r = apply_rope(q, cos, sin)
    k_slc = apply_rope(k_slc, cos, sin)
    k_win = apply_rope(k_win, cos, sin)
    q_g = q.reshape(B, T, N_KV_GROUPS, HEADS_PER_GROUP, HEAD_DIM)
    q_rg = q_r.reshape(B, T, N_KV_GROUPS, HEADS_PER_GROUP, HEAD_DIM)
    gates = gates.reshape(B, T, N_KV_GROUPS, HEADS_PER_GROUP, N_BRANCH)
    nsb = T // SEL_BLOCK
    n_sel = min(SEL_TOPN, nsb)
    nc = kc.shape[1]
    k_blocks = k_slc.reshape(B, nsb, SEL_BLOCK, N_KV_GROUPS, HEAD_DIM).transpose(0, 3, 1, 2, 4)
    v_blocks = v_slc.reshape(B, nsb, SEL_BLOCK, N_KV_GROUPS, HEAD_DIM).transpose(0, 3, 1, 2, 4)
    k_win_p = jnp.pad(k_win, ((0, 0), (WINDOW, 0), (0, 0), (0, 0)))
    v_win_p = jnp.pad(v_win, ((0, 0), (WINDOW, 0), (0, 0), (0, 0)))
    cmp_end = jnp.arange(nc) * CMP_STRIDE + CMP_LEN - 1
    blk = jnp.arange(nsb)
    scale = HEAD_DIM ** -0.5
    gather = jax.vmap(jax.vmap(lambda kb, ix: kb[ix]))
    n_keys = n_sel * SEL_BLOCK

    def block_fn(qb):
        t0 = qb * Q_BLOCK
        t = t0 + jnp.arange(Q_BLOCK)
        qc = lax.dynamic_slice_in_dim(q_g, t0, Q_BLOCK, 1)
        qr = lax.dynamic_slice_in_dim(q_rg, t0, Q_BLOCK, 1)
        g = lax.dynamic_slice_in_dim(gates, t0, Q_BLOCK, 1)
        s = jnp.einsum('bqghd,bngd->bghqn', qc, kc) * scale
        p = masked_softmax(s, cmp_end[None, :] <= t[:, None])
        o_cmp = jnp.einsum('bghqn,bngd->bqghd', p.astype(vc.dtype), vc)
        imp = block_importance(jnp.sum(p, axis=2), nsb)
        cur = t // SEL_BLOCK
        forced = (blk[None, :] == 0) | (blk[None, :] == cur[:, None]) | (blk[None, :] == cur[:, None] - 1)
        valid = blk[None, :] * SEL_BLOCK <= t[:, None]
        imp = jnp.where(forced, jnp.inf, jnp.where(valid, imp, -jnp.inf))
        _, idx = lax.top_k(imp, n_sel)
        ks = gather(k_blocks, idx).reshape(B, N_KV_GROUPS, Q_BLOCK, n_keys, HEAD_DIM)
        vs = gather(v_blocks, idx).reshape(B, N_KV_GROUPS, Q_BLOCK, n_keys, HEAD_DIM)
        kpos = (idx[..., None] * SEL_BLOCK + jnp.arange(SEL_BLOCK)).reshape(B, N_KV_GROUPS, Q_BLOCK, n_keys)
        s = jnp.einsum('bqghd,bgqkd->bghqk', qr, ks) * scale
        p = masked_softmax(s, (kpos <= t[:, None])[:, :, None])
        o_slc = jnp.einsum('bghqk,bgqkd->bqghd', p.astype(vs.dtype), vs)
        kw = lax.dynamic_slice_in_dim(k_win_p, t0, WINDOW + Q_BLOCK, 1)
        vw = lax.dynamic_slice_in_dim(v_win_p, t0, WINDOW + Q_BLOCK, 1)
        wpos = t0 - WINDOW + jnp.arange(WINDOW + Q_BLOCK)
        dist = t[:, None] - wpos[None, :]
        wmask = (dist >= 0) & (dist < WINDOW) & (wpos[None, :] >= 0)
        s = jnp.einsum('bqghd,bkgd->bghqk', qr, kw) * scale
        p = masked_softmax(s, wmask)
        o_win = jnp.einsum('bghqk,bkgd->bqghd', p.astype(vw.dtype), vw)
        return g[..., 0:1] * o_cmp + g[..., 1:2] * o_slc + g[..., 2:3] * o_win

    out = lax.map(block_fn, jnp.arange(T // Q_BLOCK))
    return out.transpose(1, 0, 2, 3, 4, 5).reshape(B, T, ATTN_WIDTH)


def s5_mixer(u, a_re, a_im, log_dt, b_re, b_im, c_re, c_im, d_skip):
    B, T = u.shape[0], u.shape[1]
    f32 = jnp.float32
    uf = u.astype(f32)
    lam = lax.complex(a_re.astype(f32), a_im.astype(f32))
    dt = jnp.exp(log_dt.astype(f32))[:, None]
    lam_bar = jnp.exp(lam * dt)
    b_bar = ((lam_bar - 1.0) / lam)[..., None] * lax.complex(b_re.astype(f32), b_im.astype(f32))
    c_mat = lax.complex(c_re.astype(f32), c_im.astype(f32))
    nch = T // SSM_CHUNK
    uc = uf.reshape(B, nch, SSM_CHUNK, SSM_GROUPS, SSM_GROUP).transpose(1, 0, 2, 3, 4)

    def binop(e1, e2):
        a1, b1 = e1
        a2, b2 = e2
        return a1 * a2, a2 * b1 + b2

    def step(carry, u_chunk):
        bu = jnp.einsum('btgc,gpc->btgp', u_chunk.astype(jnp.complex64), b_bar)
        a = jnp.broadcast_to(lam_bar, bu.shape)
        a_cum, b_cum = lax.associative_scan(binop, (a, bu), axis=1)
        s = a_cum * carry[:, None] + b_cum
        y = jnp.real(jnp.einsum('gcp,btgp->btgc', c_mat, s))
        return s[:, -1], y

    carry0 = jnp.zeros((B, SSM_GROUPS, SSM_STATE), jnp.complex64)
    _, y = lax.scan(step, carry0, uc)
    y = y.transpose(1, 0, 2, 3, 4).reshape(B, T, SSM_WIDTH)
    return y + d_skip.astype(f32) * uf


def hybrid_layer(x, c, w_ada, b_ada, norm_g, w_in, q_norm_g, k_cmp_norm_g, k_slc_norm_g, k_win_norm_g,
                 cmp_pos_k, cmp_pos_v, cmp_w1_k, cmp_w2_k, cmp_w1_v, cmp_w2_v,
                 ssm_a_re, ssm_a_im, ssm_log_dt, ssm_b_re, ssm_b_im, ssm_c_re, ssm_c_im, ssm_d,
                 glu_w, glu_b, w_out):
    B, T = x.shape[0], x.shape[1]
    mod = jax.nn.silu(c) @ w_ada + b_ada
    shift, scale, gate = jnp.split(mod, 3, axis=-1)
    h = rms_norm(x, norm_g) * (1 + scale[:, None, :]) + shift[:, None, :]
    proj = h @ w_in
    widths = [ATTN_WIDTH] + [KV_WIDTH] * 6 + [ATTN_WIDTH, N_BRANCH * N_HEADS, SSM_WIDTH, SSM_WIDTH]
    offs = [int(v) for v in np.cumsum(widths)[:-1]]
    q, kcr, vcr, ksl, vsl, kwn, vwn, z_a, g_br, u, z_s = jnp.split(proj, offs, axis=-1)

    def kv_heads(t):
        return t.reshape(B, T, N_KV_GROUPS, HEAD_DIM)

    q = rms_norm(q.reshape(B, T, N_HEADS, HEAD_DIM), q_norm_g)
    kc = rms_norm(compress(kv_heads(kcr), cmp_pos_k, cmp_w1_k, cmp_w2_k), k_cmp_norm_g)
    vc = compress(kv_heads(vcr), cmp_pos_v, cmp_w1_v, cmp_w2_v)
    k_slc = rms_norm(kv_heads(ksl), k_slc_norm_g)
    k_win = rms_norm(kv_heads(kwn), k_win_norm_g)
    gates = jax.nn.sigmoid(g_br).reshape(B, T, N_HEADS, N_BRANCH)
    attn = nsa_attention(q, kc, vc, k_slc, kv_heads(vsl), k_win, kv_heads(vwn), gates)
    attn = attn * jax.nn.silu(z_a)

    y = s5_mixer(u, ssm_a_re, ssm_a_im, ssm_log_dt, ssm_b_re, ssm_b_im, ssm_c_re, ssm_c_im, ssm_d)
    y = jax.nn.gelu(y)
    y = y * jax.nn.sigmoid(y @ glu_w.astype(jnp.float32) + glu_b.astype(jnp.float32))
    ssm = y.astype(x.dtype) * jax.nn.silu(z_s)

    mix = jnp.concatenate([attn, ssm], axis=-1) @ w_out
    return x + gate[:, None, :] * mix


def setup_inputs(seed: int = 0) -> dict:
    key = jax.random.key(seed)
    ks = jax.random.split(key, 32)
    f32 = jnp.float32

    def nrm(k, shape, s):
        return jax.random.normal(k, shape, f32) * s

    D = D_MODEL
    L = DEPTH
    ng, p, cg = SSM_GROUPS, SSM_STATE, SSM_GROUP
    return {
        "x": nrm(ks[0], (BATCH, SEQ, D), 1.0),
        "c": nrm(ks[1], (BATCH, D), 1.0),
        "w_ada": nrm(ks[2], (L, D, 3 * D), 0.5 * D ** -0.5),
        "b_ada": nrm(ks[3], (L, 3 * D), 0.01),
        "norm_g": 1.0 + nrm(ks[4], (L, D), 0.02),
        "w_in": nrm(ks[5], (L, D, IN_WIDTH), D ** -0.5),
        "q_norm_g": 1.0 + nrm(ks[6], (L, HEAD_DIM), 0.02),
        "k_cmp_norm_g": 1.0 + nrm(ks[7], (L, HEAD_DIM), 0.02),
        "k_slc_norm_g": 1.0 + nrm(ks[8], (L, HEAD_DIM), 0.02),
        "k_win_norm_g": 1.0 + nrm(ks[9], (L, HEAD_DIM), 0.02),
        "cmp_pos_k": nrm(ks[10], (L, CMP_LEN, HEAD_DIM), 0.1),
        "cmp_pos_v": nrm(ks[11], (L, CMP_LEN, HEAD_DIM), 0.1),
        "cmp_w1_k": nrm(ks[12], (L, CMP_LEN * HEAD_DIM, CMP_HIDDEN), (CMP_LEN * HEAD_DIM) ** -0.5),
        "cmp_w2_k": nrm(ks[13], (L, CMP_HIDDEN, HEAD_DIM), CMP_HIDDEN ** -0.5),
        "cmp_w1_v": nrm(ks[14], (L, CMP_LEN * HEAD_DIM, CMP_HIDDEN), (CMP_LEN * HEAD_DIM) ** -0.5),
        "cmp_w2_v": nrm(ks[15], (L, CMP_HIDDEN, HEAD_DIM), CMP_HIDDEN ** -0.5),
        "ssm_a_re": -0.5 + nrm(ks[16], (L, ng, p), 0.01),
        "ssm_a_im": math.pi * jnp.arange(p, dtype=f32)[None, None, :] + nrm(ks[17], (L, ng, p), 0.01),
        "ssm_log_dt": jax.random.uniform(ks[18], (L, ng), f32, math.log(DT_MIN), math.log(DT_MAX)),
        "ssm_b_re": nrm(ks[19], (L, ng, p, cg), (2 * cg) ** -0.5),
        "ssm_b_im": nrm(ks[20], (L, ng, p, cg), (2 * cg) ** -0.5),
        "ssm_c_re": nrm(ks[21], (L, ng, cg, p), p ** -0.5),
        "ssm_c_im": nrm(ks[22], (L, ng, cg, p), p ** -0.5),
        "ssm_d": nrm(ks[23], (L, SSM_WIDTH), 1.0),
        "glu_w": nrm(ks[24], (L, SSM_WIDTH, SSM_WIDTH), SSM_WIDTH ** -0.5),
        "glu_b": nrm(ks[25], (L, SSM_WIDTH), 0.01),
        "w_out": nrm(ks[26], (L, D, D), D ** -0.5),
    }


def reference(x, c, w_ada, b_ada, norm_g, w_in, q_norm_g, k_cmp_norm_g, k_slc_norm_g, k_win_norm_g,
              cmp_pos_k, cmp_pos_v, cmp_w1_k, cmp_w2_k, cmp_w1_v, cmp_w2_v,
              ssm_a_re, ssm_a_im, ssm_log_dt, ssm_b_re, ssm_b_im, ssm_c_re, ssm_c_im, ssm_d,
              glu_w, glu_b, w_out):
    for l in range(DEPTH):
        x = hybrid_layer(x, c, w_ada[l], b_ada[l], norm_g[l], w_in[l], q_norm_g[l], k_cmp_norm_g[l],
                         k_slc_norm_g[l], k_win_norm_g[l], cmp_pos_k[l], cmp_pos_v[l], cmp_w1_k[l],
                         cmp_w2_k[l], cmp_w1_v[l], cmp_w2_v[l], ssm_a_re[l], ssm_a_im[l], ssm_log_dt[l],
                         ssm_b_re[l], ssm_b_im[l], ssm_c_re[l], ssm_c_im[l], ssm_d[l], glu_w[l], glu_b[l],
                         w_out[l])
    return x
```

```python
import functools
import math

import jax
import jax.numpy as jnp
import numpy as np
from jax import lax
from jax.experimental import pallas as pl
from jax.experimental.pallas import tpu as pltpu

F32 = jnp.float32
BF16 = jnp.bfloat16

D_MODEL = 1024
ATTN_WIDTH = 512
N_HEADS = 8
HEAD_DIM = 64
N_KV_GROUPS = 2
HEADS_PER_GROUP = N_HEADS // N_KV_GROUPS
KV_WIDTH = N_KV_GROUPS * HEAD_DIM
CMP_LEN = 32
CMP_STRIDE = 16
CMP_HIDDEN = 256
SEL_BLOCK = 64
SEL_TOPN = 16
WINDOW = 512
N_BRANCH = 3
ROPE_THETA = 10000.0
SSM_WIDTH = D_MODEL - ATTN_WIDTH
SSM_GROUP = 16
SSM_GROUPS = SSM_WIDTH // SSM_GROUP
SSM_STATE = 64
EPS = 1e-6

LANES = 128
MAX_SEL_BLOCKS = LANES
NEG = -1e30
FORCED = 1e30
VMEM_LIMIT = 56 * 1024 * 1024

S5_STEP = 16
ROW_TILE = 512
Q_TILE = 128
K_TILE = 512

_C_Q, _C_KC, _C_VC, _C_KS, _C_VS, _C_KW, _C_VW = 0, 512, 640, 768, 896, 1024, 1152
_C_ZA, _C_U, _C_ZS, _C_GT = 1280, 1792, 2304, 2816
IN_PAD = _C_GT + N_KV_GROUPS * LANES


def _cparams(sem):
    return pltpu.CompilerParams(dimension_semantics=sem, vmem_limit_bytes=VMEM_LIMIT)


def _split_bf16(a):
    hi = a.astype(BF16)
    lo = (a - hi.astype(F32)).astype(BF16)
    return hi, lo


def _dot(a, b):
    return jnp.dot(a, b, preferred_element_type=F32)


def _dot_nt(a, b):
    return lax.dot_general(a, b, (((1,), (1,)), ((), ())), preferred_element_type=F32)


def _dot_f32(a, b):
    ah, al = _split_bf16(a)
    bh, bl = _split_bf16(b)
    return _dot(ah, bh) + (_dot(al, bh) + _dot(ah, bl))


def _adaln_kernel(c_ref, w_ref, b_ref, o_ref):
    c = c_ref[...]
    o_ref[...] = _dot_f32(jax.nn.silu(c), w_ref[...]) + b_ref[...]


def _adaln(c, w_ada, b_ada):
    B, D = c.shape
    n = w_ada.shape[1]
    tn = 1024
    return pl.pallas_call(
        _adaln_kernel,
        grid=(n // tn,),
        in_specs=[pl.BlockSpec((B, D), lambda j: (0, 0)),
                  pl.BlockSpec((D, tn), lambda j: (0, j)),
                  pl.BlockSpec((1, tn), lambda j: (0, j))],
        out_specs=pl.BlockSpec((B, tn), lambda j: (0, j)),
        out_shape=jax.ShapeDtypeStruct((B, n), F32),
        compiler_params=_cparams(("arbitrary",)),
        name="adaln",
    )(c, w_ada, b_ada.reshape(1, n))


def _head_norm(v, ones_ref, gvec):
    ss = _dot((v * v).astype(BF16), ones_ref[...])
    return v * lax.rsqrt(ss * (1.0 / HEAD_DIM) + EPS) * gvec


def _inproj_kernel(x_ref, mod_ref, ng_ref, w_ref, cos_ref, sin_ref, ones_ref, gq_ref, gks_ref, gkw_ref,
                   qn_ref, qr_ref, kc_ref, vc_ref, ksa_ref, vsa_ref, kw_ref, vwa_ref,
                   sza_ref, u_ref, szs_ref, gt_ref):
    tm = x_ref.shape[1]
    ti = pl.program_id(1)
    x = x_ref[0]
    ms = jnp.mean(x * x, axis=-1, keepdims=True)
    shift = mod_ref[0, 0:1, :]
    scale = mod_ref[0, 1:2, :]
    h = (x * lax.rsqrt(ms + EPS)) * ng_ref[...] * (1.0 + scale) + shift
    hb = h.astype(BF16)

    def proj(c0, width):
        return _dot(hb, w_ref[:, c0:c0 + width])

    cosv = cos_ref[...]
    sinv = sin_ref[...]
    lane = lax.broadcasted_iota(jnp.int32, (tm, LANES), 1)
    row = lax.broadcasted_iota(jnp.int32, (tm, LANES), 0)
    first_half = (lane & (HEAD_DIM - 1)) < (HEAD_DIM // 2)
    low = lane < HEAD_DIM

    def rope(v):
        sw = jnp.where(first_half, pltpu.roll(v, LANES - HEAD_DIM // 2, 1), pltpu.roll(v, HEAD_DIM // 2, 1))
        return v * cosv + sw * sinv

    def group_part(v, g):
        return v if g == 0 else pltpu.roll(v, HEAD_DIM, 1)

    for c in range(ATTN_WIDTH // LANES):
        qn = _head_norm(proj(_C_Q + c * LANES, LANES), ones_ref, gq_ref[...]) * (HEAD_DIM ** -0.5)
        qn_ref[0, :, c * LANES:(c + 1) * LANES] = qn.astype(BF16)
        qr_ref[0, :, c * LANES:(c + 1) * LANES] = rope(qn).astype(BF16)

    kc_ref[0] = proj(_C_KC, LANES).astype(BF16)
    vc_ref[0] = proj(_C_VC, LANES).astype(BF16)

    blk = (ti * tm + row) >> 6
    onehot = jnp.where(lane == blk, 1.0, 0.0).astype(BF16)
    ones_col = jnp.where(lane == HEAD_DIM, 1.0, 0.0)

    ksl = rope(_head_norm(proj(_C_KS, LANES), ones_ref, gks_ref[...]))
    vsl = proj(_C_VS, LANES)
    kwn = rope(_head_norm(proj(_C_KW, LANES), ones_ref, gkw_ref[...]))
    vwn = proj(_C_VW, LANES)
    for g in range(N_KV_GROUPS):
        ksa_ref[0, g, :, 0:LANES] = onehot
        ksa_ref[0, g, :, LANES:2 * LANES] = jnp.where(low, group_part(ksl, g), 0.0).astype(BF16)
        vsa_ref[0, g] = jnp.where(low, group_part(vsl, g), ones_col).astype(BF16)
        kw_ref[0, g] = jnp.where(low, group_part(kwn, g), 0.0).astype(BF16)
        vwa_ref[0, g] = jnp.where(low, group_part(vwn, g), ones_col).astype(BF16)
        gt_ref[0, g] = jax.nn.sigmoid(proj(_C_GT + g * LANES, LANES))

    sza_ref[0] = jax.nn.silu(proj(_C_ZA, ATTN_WIDTH)).astype(BF16)
    u_ref[0] = proj(_C_U, SSM_WIDTH).astype(BF16)
    szs_ref[0] = jax.nn.silu(proj(_C_ZS, SSM_WIDTH)).astype(BF16)


def _inproj(x, mod3, norm_g, w_pad, cos_t, sin_t, ones2, gq, gks, gkw):
    B, T, D = x.shape
    tm = ROW_TILE
    G = N_KV_GROUPS
    tok = lambda w: pl.BlockSpec((1, tm, w), lambda b, i: (b, i, 0))
    grp = lambda w: pl.BlockSpec((1, G, tm, w), lambda b, i: (b, 0, i, 0))
    const = lambda shape: pl.BlockSpec(shape, lambda b, i: tuple(0 for _ in shape))
    tshape = lambda w, dt: jax.ShapeDtypeStruct((B, T, w), dt)
    gshape = lambda w, dt: jax.ShapeDtypeStruct((B, G, T, w), dt)
    return pl.pallas_call(
        _inproj_kernel,
        grid=(B, T // tm),
        in_specs=[tok(D),
                  pl.BlockSpec((1, 3, D), lambda b, i: (b, 0, 0)),
                  const((1, D)),
                  const((D, IN_PAD)),
                  pl.BlockSpec((tm, LANES), lambda b, i: (i, 0)),
                  pl.BlockSpec((tm, LANES), lambda b, i: (i, 0)),
                  const((LANES, LANES)), const((1, LANES)), const((1, LANES)), const((1, LANES))],
        out_specs=[tok(ATTN_WIDTH), tok(ATTN_WIDTH), tok(LANES), tok(LANES),
                   grp(2 * LANES), grp(LANES), grp(LANES), grp(LANES),
                   tok(ATTN_WIDTH), tok(SSM_WIDTH), tok(SSM_WIDTH), grp(LANES)],
        out_shape=[tshape(ATTN_WIDTH, BF16), tshape(ATTN_WIDTH, BF16), tshape(LANES, BF16), tshape(LANES, BF16),
                   gshape(2 * LANES, BF16), gshape(LANES, BF16), gshape(LANES, BF16), gshape(LANES, BF16),
                   tshape(ATTN_WIDTH, BF16), tshape(SSM_WIDTH, BF16), tshape(SSM_WIDTH, BF16), gshape(LANES, F32)],
        compiler_params=_cparams(("parallel", "parallel")),
        name="inproj",
    )(x, mod3, norm_g, w_pad, cos_t, sin_t, ones2, gq, gks, gkw)


def _compress_kernel(kx_ref, vx_ref, wtk_ref, wbk_ref, wtv_ref, wbv_ref, w1k_ref, w1v_ref, posk_ref, posv_ref,
                     w2k_ref, w2v_ref, gk_ref, ones_ref, kc_ref, vc_ref):
    ns = kx_ref.shape[1]
    lane = lax.broadcasted_iota(jnp.int32, (ns, LANES), 1)
    row = lax.broadcasted_iota(jnp.int32, (ns, LANES), 0)
    low = lane < HEAD_DIM
    live = row < ns - 1

    def mlp(x_ref, wt_ref, wb_ref, w1_ref, pos_ref, w2_ref):
        x = x_ref[0]
        top = _dot(x, wt_ref[...])
        bot = pltpu.roll(_dot(x, wb_ref[...]), ns - 1, 0)
        b1 = _dot_f32(pos_ref[...], w1_ref[...])[0:1]
        hid = top + bot + jnp.concatenate([b1, b1], axis=1)
        return _dot(jax.nn.gelu(hid).astype(BF16), w2_ref[...])

    kc = mlp(kx_ref, wtk_ref, wbk_ref, w1k_ref, posk_ref, w2k_ref)
    kc = _head_norm(kc, ones_ref, gk_ref[...])
    vc = mlp(vx_ref, wtv_ref, wbv_ref, w1v_ref, posv_ref, w2v_ref)
    for g in range(N_KV_GROUPS):
        kg = kc if g == 0 else pltpu.roll(kc, HEAD_DIM, 1)
        vg = vc if g == 0 else pltpu.roll(vc, HEAD_DIM, 1)
        kc_ref[0, g] = jnp.where(low & live, kg, 0.0).astype(BF16)
        vc_ref[0, g] = jnp.where(low & live, vg, 0.0).astype(BF16)


def _compress(kx, vx, wtk, wbk, wtv, wbv, w1k, w1v, posk, posv, w2k, w2v, gk, ones2):
    B, ns, w = kx.shape
    G = N_KV_GROUPS
    const = lambda a: pl.BlockSpec(a.shape, lambda b: tuple(0 for _ in a.shape))
    consts = (wtk, wbk, wtv, wbv, w1k, w1v, posk, posv, w2k, w2v, gk, ones2)
    return pl.pallas_call(
        _compress_kernel,
        grid=(B,),
        in_specs=[pl.BlockSpec((1, ns, w), lambda b: (b, 0, 0)),
                  pl.BlockSpec((1, ns, w), lambda b: (b, 0, 0))] + [const(a) for a in consts],
        out_specs=[pl.BlockSpec((1, G, ns, LANES), lambda b: (b, 0, 0, 0)),
                   pl.BlockSpec((1, G, ns, LANES), lambda b: (b, 0, 0, 0))],
        out_shape=[jax.ShapeDtypeStruct((B, G, ns, LANES), BF16),
                   jax.ShapeDtypeStruct((B, G, ns, LANES), BF16)],
        compiler_params=_cparams(("parallel",)),
        name="compress",
    )(kx, vx, *consts)


def _stack_heads(q):
    return jnp.concatenate([q[:, h * HEAD_DIM:(h + 1) * HEAD_DIM] for h in range(HEADS_PER_GROUP)], axis=0)


def _cmpsel_kernel(qn_ref, kc_ref, vc_ref, wimp_ref, ocmp_ref, bias_ref):
    tq = qn_ref.shape[1]
    nc = kc_ref.shape[2]
    hg = HEADS_PER_GROUP
    t0 = pl.program_id(2) * tq
    qs = _stack_heads(qn_ref[0])
    s = _dot_nt(qs, kc_ref[0, 0, :, 0:HEAD_DIM])
    row = lax.broadcasted_iota(jnp.int32, (hg * tq, nc), 0)
    col = lax.broadcasted_iota(jnp.int32, (hg * tq, nc), 1)
    t = t0 + (row & (tq - 1))
    mask = col * CMP_STRIDE + (CMP_LEN - 1) <= t
    s = jnp.where(mask, s, -jnp.inf)
    m = jnp.max(s, axis=-1, keepdims=True)
    m = jnp.where(m > -jnp.inf, m, 0.0)
    e = jnp.where(mask, jnp.exp(s - m), 0.0)
    p = e / jnp.maximum(jnp.sum(e, axis=-1, keepdims=True), 1e-30)
    o = _dot(p.astype(BF16), vc_ref[0, 0])
    ocmp_ref[0] = jnp.concatenate([o[h * tq:(h + 1) * tq, 0:HEAD_DIM] for h in range(hg)], axis=1)

    ps = p[0:tq]
    for h in range(1, hg):
        ps = ps + p[h * tq:(h + 1) * tq]
    p0 = ps.astype(BF16)
    r1 = ps - p0.astype(F32)
    p1 = r1.astype(BF16)
    p2 = (r1 - p1.astype(F32)).astype(BF16)
    w = wimp_ref[...]
    imp = _dot_nt(w, p0) + (_dot_nt(w, p1) + _dot_nt(w, p2))

    blk = lax.broadcasted_iota(jnp.int32, (MAX_SEL_BLOCKS, tq), 0)
    tt = t0 + lax.broadcasted_iota(jnp.int32, (MAX_SEL_BLOCKS, tq), 1)
    cur = tt >> 6
    forced = (blk == 0) | (blk == cur) | (blk == cur - 1)
    valid = blk <= cur
    key = jnp.where(forced, FORCED, jnp.where(valid, imp, -1.0))
    sel = jnp.zeros((MAX_SEL_BLOCKS, tq), F32)
    for _ in range(SEL_TOPN):
        mx = jnp.max(key, axis=0, keepdims=True)
        first = jnp.min(jnp.where(key == mx, blk, MAX_SEL_BLOCKS), axis=0, keepdims=True)
        hit = blk == first
        sel = jnp.where(hit, 1.0, sel)
        key = jnp.where(hit, -2.0, key)
    bias_t = jnp.where((sel > 0.5) & valid, 0.0, NEG)
    bias_ref[0, 0] = jnp.transpose(bias_t).astype(BF16)


def _cmpsel(q_n, kc, vc, wimp_t):
    B, T, _ = q_n.shape
    G = N_KV_GROUPS
    nc = kc.shape[2]
    tq = Q_TILE
    gw = HEADS_PER_GROUP * HEAD_DIM
    return pl.pallas_call(
        _cmpsel_kernel,
        grid=(B, G, T // tq),
        in_specs=[pl.BlockSpec((1, tq, gw), lambda b, g, i: (b, i, g)),
                  pl.BlockSpec((1, 1, nc, LANES), lambda b, g, i: (b, g, 0, 0)),
                  pl.BlockSpec((1, 1, nc, LANES), lambda b, g, i: (b, g, 0, 0)),
                  pl.BlockSpec((MAX_SEL_BLOCKS, nc), lambda b, g, i: (0, 0))],
        out_specs=[pl.BlockSpec((1, tq, gw), lambda b, g, i: (b, i, g)),
                   pl.BlockSpec((1, 1, tq, LANES), lambda b, g, i: (b, g, i, 0))],
        out_shape=[jax.ShapeDtypeStruct((B, T, ATTN_WIDTH), F32),
                   jax.ShapeDtypeStruct((B, G, T, LANES), BF16)],
        compiler_params=_cparams(("parallel", "parallel", "parallel")),
        name="cmpsel",
    )(q_n, kc, vc, wimp_t)


def _attn_kernel(qr_ref, bias_ref, ks_ref, vs_ref, kw_ref, vw_ref, ocmp_ref, gt_ref, sza_ref, out_ref,
                 qa_ref, m_ref, acc_ref):
    tq = qr_ref.shape[1]
    hg = HEADS_PER_GROUP
    rows = hg * tq
    tk = K_TILE
    t0 = pl.program_id(2) * tq
    q = qr_ref[0]
    bias = bias_ref[0, 0]
    zpad = jnp.zeros((tq, LANES - HEAD_DIM), BF16)
    for h in range(hg):
        qa_ref[h * tq:(h + 1) * tq, 0:LANES] = bias
        qa_ref[h * tq:(h + 1) * tq, LANES:2 * LANES] = jnp.concatenate(
            [q[:, h * HEAD_DIM:(h + 1) * HEAD_DIM], zpad], axis=1)
    qa = qa_ref[...]
    m_ref[...] = jnp.full(m_ref.shape, NEG, F32)
    acc_ref[...] = jnp.zeros(acc_ref.shape, F32)
    t_row = t0 + (lax.broadcasted_iota(jnp.int32, (rows, 1), 0) & (tq - 1))

    def sel_tile(kt, diagonal):
        k0 = pl.multiple_of(kt * tk, tk)
        s = _dot_nt(qa, ks_ref[0, 0, pl.ds(k0, tk), :])
        if diagonal:
            kpos = k0 + lax.broadcasted_iota(jnp.int32, (rows, tk), 1)
            s = jnp.where(kpos <= t_row, s, NEG)
        m_prev = m_ref[...]
        m_new = jnp.maximum(m_prev, jnp.max(s, axis=-1, keepdims=True))
        alpha = jnp.exp(m_prev - m_new)
        p = jnp.exp(s - m_new[:, 0:1])
        acc_ref[...] = acc_ref[...] * alpha + _dot(p.astype(BF16), vs_ref[0, 0, pl.ds(k0, tk), :])
        m_ref[...] = m_new

    kt_last = t0 // tk

    def body(kt, carry):
        sel_tile(kt, False)
        return carry

    lax.fori_loop(0, kt_last, body, 0)
    sel_tile(kt_last, True)
    acc = acc_ref[...]
    o_slc = acc[:, 0:HEAD_DIM] / acc[:, HEAD_DIM:HEAD_DIM + 1]

    wk = WINDOW + tq
    ws = pl.multiple_of(jnp.maximum(t0 - WINDOW, 0), tq)
    qs = qa[:, LANES:LANES + HEAD_DIM]
    s = _dot_nt(qs, kw_ref[0, 0, pl.ds(ws, wk), 0:HEAD_DIM])
    dist = t_row - (ws + lax.broadcasted_iota(jnp.int32, (rows, wk), 1))
    s = jnp.where((dist >= 0) & (dist < WINDOW), s, NEG)
    p = jnp.exp(s - jnp.max(s, axis=-1, keepdims=True))
    ow = _dot(p.astype(BF16), vw_ref[0, 0, pl.ds(ws, wk), :])
    o_win = ow[:, 0:HEAD_DIM] / ow[:, HEAD_DIM:HEAD_DIM + 1]

    gt = gt_ref[0, 0]
    oc = ocmp_ref[0]
    parts = []
    for h in range(hg):
        g0 = gt[:, N_BRANCH * h:N_BRANCH * h + 1]
        g1 = gt[:, N_BRANCH * h + 1:N_BRANCH * h + 2]
        g2 = gt[:, N_BRANCH * h + 2:N_BRANCH * h + 3]
        parts.append(g0 * oc[:, h * HEAD_DIM:(h + 1) * HEAD_DIM]
                     + g1 * o_slc[h * tq:(h + 1) * tq] + g2 * o_win[h * tq:(h + 1) * tq])
    attn = jnp.concatenate(parts, axis=1) * sza_ref[0].astype(F32)
    out_ref[0] = attn.astype(BF16)


def _attn(q_r, bias, ks_aug, vs_aug, kw, vw_aug, o_cmp, gates, sza):
    B, T, _ = q_r.shape
    G = N_KV_GROUPS
    tq = Q_TILE
    gw = HEADS_PER_GROUP * HEAD_DIM
    rows = HEADS_PER_GROUP * tq
    tokg = lambda: pl.BlockSpec((1, tq, gw), lambda b, g, i: (b, i, g))
    grp = lambda w: pl.BlockSpec((1, 1, tq, w), lambda b, g, i: (b, g, i, 0))
    res = lambda w: pl.BlockSpec((1, 1, T, w), lambda b, g, i: (b, g, 0, 0))
    return pl.pallas_call(
        _attn_kernel,
        grid=(B, G, T // tq),
        in_specs=[tokg(), grp(LANES), res(2 * LANES), res(LANES), res(LANES), res(LANES),
                  tokg(), grp(LANES), tokg()],
        out_specs=tokg(),
        out_shape=jax.ShapeDtypeStruct((B, T, ATTN_WIDTH), BF16),
        scratch_shapes=[pltpu.VMEM((rows, 2 * LANES), BF16),
                        pltpu.VMEM((rows, LANES), F32),
                        pltpu.VMEM((rows, LANES), F32)],
        compiler_params=_cparams(("parallel", "parallel", "arbitrary")),
        name="attn",
    )(q_r, bias, ks_aug, vs_aug, kw, vw_aug, o_cmp, gates, sza)


def _s5_state_kernel(u_ref, rh_ref, rl_ref, z_ref):
    u = u_ref[0]
    z_ref[...] = _dot(u, rh_ref[0]) + _dot(u, rl_ref[0])


def _s5_scan_kernel(z_ref, a1_ref, a2_ref, s_ref):
    nch = z_ref.shape[0]
    a1 = a1_ref[...]
    a2 = a2_ref[...]
    nl = z_ref.shape[2] // LANES

    def step(c, s):
        s_ref[c] = s
        sw = jnp.concatenate([pltpu.roll(s[:, j * LANES:(j + 1) * LANES], SSM_STATE, 1) for j in range(nl)],
                             axis=1)
        return a1 * s + a2 * sw + z_ref[c]

    lax.fori_loop(0, nch, step, jnp.zeros(z_ref.shape[1:], F32))


def _s5_out_kernel(u_ref, s_ref, mh_ref, ml_ref, oh_ref, ol_ref, y_ref):
    u = u_ref[0]
    sh, sl = _split_bf16(s_ref[...])
    y = _dot(u, mh_ref[0]) + _dot(u, ml_ref[0])
    y_ref[...] = y + (_dot(sh, oh_ref[0]) + (_dot(sl, oh_ref[0]) + _dot(sh, ol_ref[0])))


def _s5(u_g, r_hi, r_lo, m_hi, m_lo, o_hi, o_lo, a1, a2, B):
    ng, rows, cw = u_g.shape
    sw = 2 * SSM_STATE
    nch = rows // B
    per_g = lambda shape: pl.BlockSpec((1,) + shape, lambda g: (g, 0, 0))
    z = pl.pallas_call(
        _s5_state_kernel,
        grid=(ng,),
        in_specs=[per_g((rows, cw)), per_g((cw, sw)), per_g((cw, sw))],
        out_specs=pl.BlockSpec((rows, sw), lambda g: (0, g)),
        out_shape=jax.ShapeDtypeStruct((rows, ng * sw), F32),
        compiler_params=_cparams(("parallel",)),
        name="s5_state",
    )(u_g, r_hi, r_lo)
    lb = 4 * sw
    s = pl.pallas_call(
        _s5_scan_kernel,
        grid=(ng * sw // lb,),
        in_specs=[pl.BlockSpec((nch, B, lb), lambda j: (0, 0, j)),
                  pl.BlockSpec((1, lb), lambda j: (0, j)),
                  pl.BlockSpec((1, lb), lambda j: (0, j))],
        out_specs=pl.BlockSpec((nch, B, lb), lambda j: (0, 0, j)),
        out_shape=jax.ShapeDtypeStruct((nch, B, ng * sw), F32),
        compiler_params=_cparams(("parallel",)),
        name="s5_scan",
    )(z.reshape(nch, B, ng * sw), a1, a2)
    return pl.pallas_call(
        _s5_out_kernel,
        grid=(ng,),
        in_specs=[per_g((rows, cw)), pl.BlockSpec((rows, sw), lambda g: (0, g)),
                  per_g((cw, cw)), per_g((cw, cw)), per_g((sw, cw)), per_g((sw, cw))],
        out_specs=pl.BlockSpec((rows, cw), lambda g: (0, g)),
        out_shape=jax.ShapeDtypeStruct((rows, ng * cw), F32),
        compiler_params=_cparams(("parallel",)),
        name="s5_out",
    )(u_g, s.reshape(rows, ng * sw), m_hi, m_lo, o_hi, o_lo)


def _cmul(ar, ai, br, bi):
    return ar * br - ai * bi, ar * bi + ai * br


def _s5_operators(a_re, a_im, log_dt, b_re, b_im, c_re, c_im):
    hp = lax.Precision.HIGHEST
    L = S5_STEP
    dt = jnp.exp(log_dt)[:, None]
    mag = jnp.exp(a_re * dt)
    lr, li = mag * jnp.cos(a_im * dt), mag * jnp.sin(a_im * dt)
    den = a_re * a_re + a_im * a_im
    fr, fi = _cmul(lr - 1.0, li, a_re / den, -a_im / den)
    bbr, bbi = _cmul(fr[..., None], fi[..., None], b_re, b_im)
    pr, pi = [jnp.ones_like(lr)], [jnp.zeros_like(li)]
    for _ in range(L):
        nr, ni = _cmul(pr[-1], pi[-1], lr, li)
        pr.append(nr)
        pi.append(ni)
    pr, pi = jnp.stack(pr), jnp.stack(pi)
    cpr, cpi = _cmul(c_re[None], c_im[None], pr[:L, :, None, :], pi[:L, :, None, :])
    kd = (jnp.einsum('dgcp,gpe->dgce', cpr, bbr, precision=hp)
          - jnp.einsum('dgcp,gpe->dgce', cpi, bbi, precision=hp))
    lag = jnp.arange(L)[None, :] - jnp.arange(L)[:, None]
    km = jnp.where((lag >= 0)[:, :, None, None, None], kd[jnp.clip(lag, 0, L - 1)], 0.0)
    ng, cg = b_re.shape[0], b_re.shape[2]
    m_op = km.transpose(2, 0, 4, 1, 3).reshape(ng, L * cg, L * cg)
    rr, ri = _cmul(pr[L - 1 - jnp.arange(L)][..., None], pi[L - 1 - jnp.arange(L)][..., None],
                   bbr[None], bbi[None])
    r_op = jnp.concatenate([rr, ri], axis=2).transpose(1, 0, 3, 2).reshape(ng, L * cg, 2 * SSM_STATE)
    orr, oii = _cmul(c_re[None], c_im[None], pr[1:, :, None, :], pi[1:, :, None, :])
    o_op = jnp.concatenate([orr, -oii], axis=3).transpose(1, 3, 0, 2).reshape(ng, 2 * SSM_STATE, L * cg)
    a1 = jnp.concatenate([pr[L], pr[L]], axis=1).reshape(1, -1)
    a2 = jnp.concatenate([-pi[L], pi[L]], axis=1).reshape(1, -1)
    return m_op, r_op, o_op, a1, a2


def _final_kernel(attn_ref, y_ref, u_ref, szs_ref, x_ref, mod_ref, d_ref, gw_ref, gb_ref, wo_ref, o_ref):
    yv = y_ref[0] + d_ref[...] * u_ref[0].astype(F32)
    yg = jax.nn.gelu(yv)
    gl = jax.nn.sigmoid(_dot(yg.astype(BF16), gw_ref[...]) + gb_ref[...])
    ssm = (yg * gl) * szs_ref[0].astype(F32)
    mix = _dot(attn_ref[0], wo_ref[0:ATTN_WIDTH, :]) + _dot(ssm.astype(BF16), wo_ref[ATTN_WIDTH:D_MODEL, :])
    o_ref[0] = x_ref[0] + mod_ref[0, 2:3, :] * mix


def _final(attn, y, u, szs, x, mod3, d_skip, glu_w, glu_b, w_out):
    B, T, D = x.shape
    tm = ROW_TILE
    tok = lambda w: pl.BlockSpec((1, tm, w), lambda b, i: (b, i, 0))
    const = lambda shape: pl.BlockSpec(shape, lambda b, i: tuple(0 for _ in shape))
    return pl.pallas_call(
        _final_kernel,
        grid=(B, T // tm),
        in_specs=[tok(ATTN_WIDTH), tok(SSM_WIDTH), tok(SSM_WIDTH), tok(SSM_WIDTH), tok(D),
                  pl.BlockSpec((1, 3, D), lambda b, i: (b, 0, 0)),
                  const((1, SSM_WIDTH)), const((SSM_WIDTH, SSM_WIDTH)), const((1, SSM_WIDTH)), const((D, D))],
        out_specs=tok(D),
        out_shape=jax.ShapeDtypeStruct((B, T, D), F32),
        compiler_params=_cparams(("parallel", "parallel")),
        name="final",
    )(attn, y, u, szs, x, mod3, d_skip, glu_w, glu_b, w_out)


def _rope_tables(T):
    inv = 1.0 / (ROPE_THETA ** (jnp.arange(0, HEAD_DIM, 2, dtype=F32) / HEAD_DIM))
    ang = jnp.arange(T, dtype=F32)[:, None] * inv[None, :]
    cos, sin = jnp.cos(ang), jnp.sin(ang)
    reps = LANES // HEAD_DIM
    return jnp.tile(jnp.concatenate([cos, cos], axis=1), (1, reps)), jnp.tile(jnp.concatenate([-sin, sin], axis=1), (1, reps))


def _importance_weights(nc):
    r = SEL_BLOCK // CMP_STRIDE
    ov = CMP_LEN // CMP_STRIDE
    w = np.zeros((MAX_SEL_BLOCKS, nc), np.float32)
    for j in range(MAX_SEL_BLOCKS):
        for m in range(r):
            for n in range(ov):
                i = r * j + m - n
                if 0 <= i < nc:
                    w[j, i] += 1.0
    return jnp.asarray(w, BF16)


def _layer(x, c, w_ada, b_ada, norm_g, w_in, q_norm_g, k_cmp_norm_g, k_slc_norm_g, k_win_norm_g,
           cmp_pos_k, cmp_pos_v, cmp_w1_k, cmp_w2_k, cmp_w1_v, cmp_w2_v,
           ssm_a_re, ssm_a_im, ssm_log_dt, ssm_b_re, ssm_b_im, ssm_c_re, ssm_c_im, ssm_d,
           glu_w, glu_b, w_out):
    B, T, D = x.shape
    G = N_KV_GROUPS
    assert D == D_MODEL and T % K_TILE == 0 and T >= WINDOW + Q_TILE and T // SEL_BLOCK <= MAX_SEL_BLOCKS
    assert T // SEL_BLOCK >= SEL_TOPN

    mod3 = _adaln(c, w_ada, b_ada).reshape(B, 3, D)

    o_gbr = ATTN_WIDTH + 6 * KV_WIDTH + ATTN_WIDTH
    o_u = o_gbr + N_BRANCH * N_HEADS
    npg = N_BRANCH * HEADS_PER_GROUP
    gate_cols = [jnp.pad(w_in[:, o_gbr + g * npg:o_gbr + (g + 1) * npg], ((0, 0), (0, LANES - npg))) for g in range(G)]
    w_pad = jnp.concatenate([w_in[:, :o_gbr], w_in[:, o_u:]] + gate_cols, axis=1).astype(BF16)
    assert w_pad.shape[1] == IN_PAD

    cos_t, sin_t = _rope_tables(T)
    hh = np.arange(LANES) // HEAD_DIM
    ones2 = jnp.asarray(hh[:, None] == hh[None, :], BF16)
    tile2 = lambda g: jnp.tile(g, LANES // HEAD_DIM).reshape(1, LANES)
    (q_n, q_r, kcr, vcr, ks_aug, vs_aug, kw, vw_aug, sza, u, szs, gates) = _inproj(
        x, mod3, norm_g.reshape(1, D), w_pad, cos_t, sin_t, ones2,
        tile2(q_norm_g), tile2(k_slc_norm_g), tile2(k_win_norm_g))

    ns = T // CMP_STRIDE
    half = CMP_STRIDE * HEAD_DIM

    def seg_weight(w1_half):
        w4 = w1_half.reshape(CMP_STRIDE, 1, HEAD_DIM, 1, CMP_HIDDEN)
        eye = jnp.eye(G, dtype=F32).reshape(1, G, 1, G, 1)
        return (w4 * eye).reshape(CMP_STRIDE * G * HEAD_DIM, G * CMP_HIDDEN).astype(BF16)

    def out_weight(w2):
        eye = jnp.eye(G, dtype=F32).reshape(G, 1, G, 1)
        return (w2.reshape(1, CMP_HIDDEN, 1, HEAD_DIM) * eye).reshape(G * CMP_HIDDEN, G * HEAD_DIM).astype(BF16)

    pos_rows = lambda p: jnp.broadcast_to(p.reshape(1, CMP_LEN * HEAD_DIM), (8, CMP_LEN * HEAD_DIM))
    kc, vc = _compress(
        kcr.reshape(B, ns, CMP_STRIDE * KV_WIDTH), vcr.reshape(B, ns, CMP_STRIDE * KV_WIDTH),
        seg_weight(cmp_w1_k[:half]), seg_weight(cmp_w1_k[half:]),
        seg_weight(cmp_w1_v[:half]), seg_weight(cmp_w1_v[half:]),
        cmp_w1_k, cmp_w1_v, pos_rows(cmp_pos_k), pos_rows(cmp_pos_v),
        out_weight(cmp_w2_k), out_weight(cmp_w2_v), tile2(k_cmp_norm_g), ones2)

    o_cmp, bias = _cmpsel(q_n, kc, vc, _importance_weights(ns))
    attn = _attn(q_r, bias, ks_aug, vs_aug, kw, vw_aug, o_cmp, gates, sza)

    m_op, r_op, o_op, a1, a2 = _s5_operators(ssm_a_re, ssm_a_im, ssm_log_dt, ssm_b_re, ssm_b_im, ssm_c_re, ssm_c_im)
    nch = T // S5_STEP
    u_g = u.reshape(B, nch, S5_STEP, SSM_GROUPS, SSM_GROUP).transpose(3, 1, 0, 2, 4)
    u_g = u_g.reshape(SSM_GROUPS, nch * B, S5_STEP * SSM_GROUP)
    y_g = _s5(u_g, *_split_bf16(r_op), *_split_bf16(m_op), *_split_bf16(o_op), a1, a2, B)
    y = y_g.reshape(nch, B, SSM_GROUPS, S5_STEP, SSM_GROUP).transpose(1, 0, 3, 2, 4).reshape(B, T, SSM_WIDTH)

    return _final(attn, y, u, szs, x, mod3, ssm_d.reshape(1, SSM_WIDTH), glu_w.astype(BF16),
                  glu_b.reshape(1, SSM_WIDTH), w_out.astype(BF16))


def kernel(x, c, w_ada, b_ada, norm_g, w_in, q_norm_g, k_cmp_norm_g, k_slc_norm_g, k_win_norm_g, cmp_pos_k, cmp_pos_v, cmp_w1_k, cmp_w2_k, cmp_w1_v, cmp_w2_v, ssm_a_re, ssm_a_im, ssm_log_dt, ssm_b_re, ssm_b_im, ssm_c_re, ssm_c_im, ssm_d, glu_w, glu_b, w_out):
    params = (w_ada, b_ada, norm_g, w_in, q_norm_g, k_cmp_norm_g, k_slc_norm_g, k_win_norm_g, cmp_pos_k, cmp_pos_v,
              cmp_w1_k, cmp_w2_k, cmp_w1_v, cmp_w2_v, ssm_a_re, ssm_a_im, ssm_log_dt, ssm_b_re, ssm_b_im,
              ssm_c_re, ssm_c_im, ssm_d, glu_w, glu_b, w_out)
    for l in range(w_ada.shape[0]):
        x = _layer(x, c, *(p[l] for p in params))
    return x
```

```python
import functools
import math

import jax
import jax.numpy as jnp
import numpy as np
from jax import lax
from jax.experimental import pallas as pl
from jax.experimental.pallas import tpu as pltpu

F32 = jnp.float32
BF16 = jnp.bfloat16

D_MODEL = 1024
ATTN_WIDTH = 512
N_HEADS = 8
HEAD_DIM = 64
N_KV_GROUPS = 2
HEADS_PER_GROUP = N_HEADS // N_KV_GROUPS
KV_WIDTH = N_KV_GROUPS * HEAD_DIM
CMP_LEN = 32
CMP_STRIDE = 16
CMP_HIDDEN = 256
SEL_BLOCK = 64
SEL_TOPN = 16
WINDOW = 512
N_BRANCH = 3
ROPE_THETA = 10000.0
SSM_WIDTH = D_MODEL - ATTN_WIDTH
SSM_GROUP = 16
SSM_GROUPS = SSM_WIDTH // SSM_GROUP
SSM_STATE = 64
EPS = 1e-6

LANES = 128
MAX_SEL_BLOCKS = LANES
NEG = -1e30
FORCED = 1e30
VMEM_LIMIT = 56 * 1024 * 1024

S5_STEP = 16
ROW_TILE = 512
Q_TILE = 128
K_TILE = 512

_C_Q, _C_KC, _C_VC, _C_KS, _C_VS, _C_KW, _C_VW = 0, 512, 640, 768, 896, 1024, 1152
_C_ZA, _C_U, _C_ZS, _C_GT = 1280, 1792, 2304, 2816
IN_PAD = _C_GT + N_KV_GROUPS * LANES


def _cparams(sem):
    return pltpu.CompilerParams(dimension_semantics=sem, vmem_limit_bytes=VMEM_LIMIT)


def _split_bf16(a):
    hi = a.astype(BF16)
    lo = (a - hi.astype(F32)).astype(BF16)
    return hi, lo


def _dot(a, b):
    return jnp.dot(a, b, preferred_element_type=F32)


def _dot_nt(a, b):
    return lax.dot_general(a, b, (((1,), (1,)), ((), ())), preferred_element_type=F32)


def _dot_f32(a, b):
    ah, al = _split_bf16(a)
    bh, bl = _split_bf16(b)
    return _dot(ah, bh) + (_dot(al, bh) + _dot(ah, bl))


def _adaln_kernel(c_ref, w_ref, b_ref, o_ref):
    c = c_ref[...]
    o_ref[...] = _dot_f32(jax.nn.silu(c), w_ref[...]) + b_ref[...]


def _adaln(c, w_ada, b_ada):
    B, D = c.shape
    n = w_ada.shape[1]
    tn = 1024
    return pl.pallas_call(
        _adaln_kernel,
        grid=(n // tn,),
        in_specs=[pl.BlockSpec((B, D), lambda j: (0, 0)),
                  pl.BlockSpec((D, tn), lambda j: (0, j)),
                  pl.BlockSpec((1, tn), lambda j: (0, j))],
        out_specs=pl.BlockSpec((B, tn), lambda j: (0, j)),
        out_shape=jax.ShapeDtypeStruct((B, n), F32),
        compiler_params=_cparams(("arbitrary",)),
        name="adaln",
    )(c, w_ada, b_ada.reshape(1, n))


def _head_norm(v, ones_ref, gvec):
    ss = _dot((v * v).astype(BF16), ones_ref[...])
    return v * lax.rsqrt(ss * (1.0 / HEAD_DIM) + EPS) * gvec


def _inproj_kernel(x_ref, mod_ref, ng_ref, w_ref, cos_ref, sin_ref, ones_ref, gq_ref, gks_ref, gkw_ref,
                   qn_ref, qr_ref, kc_ref, vc_ref, ksa_ref, vsa_ref, kw_ref, vwa_ref,
                   sza_ref, u_ref, szs_ref, gt_ref):
    tm = x_ref.shape[1]
    ti = pl.program_id(1)
    x = x_ref[0]
    ms = jnp.mean(x * x, axis=-1, keepdims=True)
    shift = mod_ref[0, 0:1, :]
    scale = mod_ref[0, 1:2, :]
    h = (x * lax.rsqrt(ms + EPS)) * ng_ref[...] * (1.0 + scale) + shift
    hb = h.astype(BF16)

    def proj(c0, width):
        return _dot(hb, w_ref[:, c0:c0 + width])

    cosv = cos_ref[...]
    sinv = sin_ref[...]
    lane = lax.broadcasted_iota(jnp.int32, (tm, LANES), 1)
    row = lax.broadcasted_iota(jnp.int32, (tm, LANES), 0)
    first_half = (lane & (HEAD_DIM - 1)) < (HEAD_DIM // 2)
    low = lane < HEAD_DIM

    def rope(v):
        sw = jnp.where(first_half, pltpu.roll(v, LANES - HEAD_DIM // 2, 1), pltpu.roll(v, HEAD_DIM // 2, 1))
        return v * cosv + sw * sinv

    def group_part(v, g):
        return v if g == 0 else pltpu.roll(v, HEAD_DIM, 1)

    for c in range(ATTN_WIDTH // LANES):
        qn = _head_norm(proj(_C_Q + c * LANES, LANES), ones_ref, gq_ref[...]) * (HEAD_DIM ** -0.5)
        qn_ref[0, :, c * LANES:(c + 1) * LANES] = qn.astype(BF16)
        qr_ref[0, :, c * LANES:(c + 1) * LANES] = rope(qn).astype(BF16)

    kc_ref[0] = proj(_C_KC, LANES).astype(BF16)
    vc_ref[0] = proj(_C_VC, LANES).astype(BF16)

    blk = (ti * tm + row) >> 6
    onehot = jnp.where(lane == blk, 1.0, 0.0).astype(BF16)
    ones_col = jnp.where(lane == HEAD_DIM, 1.0, 0.0)

    ksl = rope(_head_norm(proj(_C_KS, LANES), ones_ref, gks_ref[...]))
    vsl = proj(_C_VS, LANES)
    kwn = rope(_head_norm(proj(_C_KW, LANES), ones_ref, gkw_ref[...]))
    vwn = proj(_C_VW, LANES)
    for g in range(N_KV_GROUPS):
        ksa_ref[0, g, :, 0:LANES] = onehot
        ksa_ref[0, g, :, LANES:2 * LANES] = jnp.where(low, group_part(ksl, g), 0.0).astype(BF16)
        vsa_ref[0, g] = jnp.where(low, group_part(vsl, g), ones_col).astype(BF16)
        kw_ref[0, g] = jnp.where(low, group_part(kwn, g), 0.0).astype(BF16)
        vwa_ref[0, g] = jnp.where(low, group_part(vwn, g), ones_col).astype(BF16)
        gt_ref[0, g] = jax.nn.sigmoid(proj(_C_GT + g * LANES, LANES))

    sza_ref[0] = jax.nn.silu(proj(_C_ZA, ATTN_WIDTH)).astype(BF16)
    u_ref[0] = proj(_C_U, SSM_WIDTH).astype(BF16)
    szs_ref[0] = jax.nn.silu(proj(_C_ZS, SSM_WIDTH)).astype(BF16)


def _inproj(x, mod3, norm_g, w_pad, cos_t, sin_t, ones2, gq, gks, gkw):
    B, T, D = x.shape
    tm = ROW_TILE
    G = N_KV_GROUPS
    tok = lambda w: pl.BlockSpec((1, tm, w), lambda b, i: (b, i, 0))
    grp = lambda w: pl.BlockSpec((1, G, tm, w), lambda b, i: (b, 0, i, 0))
    const = lambda shape: pl.BlockSpec(shape, lambda b, i: tuple(0 for _ in shape))
    tshape = lambda w, dt: jax.ShapeDtypeStruct((B, T, w), dt)
    gshape = lambda w, dt: jax.ShapeDtypeStruct((B, G, T, w), dt)
    return pl.pallas_call(
        _inproj_kernel,
        grid=(B, T // tm),
        in_specs=[tok(D),
                  pl.BlockSpec((1, 3, D), lambda b, i: (b, 0, 0)),
                  const((1, D)),
                  const((D, IN_PAD)),
                  pl.BlockSpec((tm, LANES), lambda b, i: (i, 0)),
                  pl.BlockSpec((tm, LANES), lambda b, i: (i, 0)),
                  const((LANES, LANES)), const((1, LANES)), const((1, LANES)), const((1, LANES))],
        out_specs=[tok(ATTN_WIDTH), tok(ATTN_WIDTH), tok(LANES), tok(LANES),
                   grp(2 * LANES), grp(LANES), grp(LANES), grp(LANES),
                   tok(ATTN_WIDTH), tok(SSM_WIDTH), tok(SSM_WIDTH), grp(LANES)],
        out_shape=[tshape(ATTN_WIDTH, BF16), tshape(ATTN_WIDTH, BF16), tshape(LANES, BF16), tshape(LANES, BF16),
                   gshape(2 * LANES, BF16), gshape(LANES, BF16), gshape(LANES, BF16), gshape(LANES, BF16),
                   tshape(ATTN_WIDTH, BF16), tshape(SSM_WIDTH, BF16), tshape(SSM_WIDTH, BF16), gshape(LANES, F32)],
        compiler_params=_cparams(("parallel", "parallel")),
        name="inproj",
    )(x, mod3, norm_g, w_pad, cos_t, sin_t, ones2, gq, gks, gkw)


def _compress_kernel(kx_ref, vx_ref, wtk_ref, wbk_ref, wtv_ref, wbv_ref, w1k_ref, w1v_ref, posk_ref, posv_ref,
                     w2k_ref, w2v_ref, gk_ref, ones_ref, kc_ref, vc_ref):
    ns = kx_ref.shape[1]
    lane = lax.broadcasted_iota(jnp.int32, (ns, LANES), 1)
    row = lax.broadcasted_iota(jnp.int32, (ns, LANES), 0)
    low = lane < HEAD_DIM
    live = row < ns - 1

    def mlp(x_ref, wt_ref, wb_ref, w1_ref, pos_ref, w2_ref):
        x = x_ref[0]
        top = _dot(x, wt_ref[...])
        bot = pltpu.roll(_dot(x, wb_ref[...]), ns - 1, 0)
        b1 = _dot_f32(pos_ref[...], w1_ref[...])[0:1]
        hid = top + bot + jnp.concatenate([b1, b1], axis=1)
        return _dot(jax.nn.gelu(hid).astype(BF16), w2_ref[...])

    kc = mlp(kx_ref, wtk_ref, wbk_ref, w1k_ref, posk_ref, w2k_ref)
    kc = _head_norm(kc, ones_ref, gk_ref[...])
    vc = mlp(vx_ref, wtv_ref, wbv_ref, w1v_ref, posv_ref, w2v_ref)
    for g in range(N_KV_GROUPS):
        kg = kc if g == 0 else pltpu.roll(kc, HEAD_DIM, 1)
        vg = vc if g == 0 else pltpu.roll(vc, HEAD_DIM, 1)
        kc_ref[0, g] = jnp.where(low & live, kg, 0.0).astype(BF16)
        vc_ref[0, g] = jnp.where(low & live, vg, 0.0).astype(BF16)


def _compress(kx, vx, wtk, wbk, wtv, wbv, w1k, w1v, posk, posv, w2k, w2v, gk, ones2):
    B, ns, w = kx.shape
    G = N_KV_GROUPS
    const = lambda a: pl.BlockSpec(a.shape, lambda b: tuple(0 for _ in a.shape))
    consts = (wtk, wbk, wtv, wbv, w1k, w1v, posk, posv, w2k, w2v, gk, ones2)
    return pl.pallas_call(
        _compress_kernel,
        grid=(B,),
        in_specs=[pl.BlockSpec((1, ns, w), lambda b: (b, 0, 0)),
                  pl.BlockSpec((1, ns, w), lambda b: (b, 0, 0))] + [const(a) for a in consts],
        out_specs=[pl.BlockSpec((1, G, ns, LANES), lambda b: (b, 0, 0, 0)),
                   pl.BlockSpec((1, G, ns, LANES), lambda b: (b, 0, 0, 0))],
        out_shape=[jax.ShapeDtypeStruct((B, G, ns, LANES), BF16),
                   jax.ShapeDtypeStruct((B, G, ns, LANES), BF16)],
        compiler_params=_cparams(("parallel",)),
        name="compress",
    )(kx, vx, *consts)


def _stack_heads(q):
    return jnp.concatenate([q[:, h * HEAD_DIM:(h + 1) * HEAD_DIM] for h in range(HEADS_PER_GROUP)], axis=0)


def _cmpsel_kernel(qn_ref, kc_ref, vc_ref, wimp_ref, ocmp_ref, bias_ref):
    tq = qn_ref.shape[1]
    nc = kc_ref.shape[2]
    hg = HEADS_PER_GROUP
    t0 = pl.program_id(2) * tq
    qs = _stack_heads(qn_ref[0])
    s = _dot_nt(qs, kc_ref[0, 0, :, 0:HEAD_DIM])
    row = lax.broadcasted_iota(jnp.int32, (hg * tq, nc), 0)
    col = lax.broadcasted_iota(jnp.int32, (hg * tq, nc), 1)
    t = t0 + (row & (tq - 1))
    mask = col * CMP_STRIDE + (CMP_LEN - 1) <= t
    s = jnp.where(mask, s, -jnp.inf)
    m = jnp.max(s, axis=-1, keepdims=True)
    m = jnp.where(m > -jnp.inf, m, 0.0)
    e = jnp.where(mask, jnp.exp(s - m), 0.0)
    p = e / jnp.maximum(jnp.sum(e, axis=-1, keepdims=True), 1e-30)
    o = _dot(p.astype(BF16), vc_ref[0, 0])
    ocmp_ref[0] = jnp.concatenate([o[h * tq:(h + 1) * tq, 0:HEAD_DIM] for h in range(hg)], axis=1)

    ps = p[0:tq]
    for h in range(1, hg):
        ps = ps + p[h * tq:(h + 1) * tq]
    p0 = ps.astype(BF16)
    r1 = ps - p0.astype(F32)
    p1 = r1.astype(BF16)
    p2 = (r1 - p1.astype(F32)).astype(BF16)
    w = wimp_ref[...]
    imp = _dot_nt(w, p0) + (_dot_nt(w, p1) + _dot_nt(w, p2))

    blk = lax.broadcasted_iota(jnp.int32, (MAX_SEL_BLOCKS, tq), 0)
    tt = t0 + lax.broadcasted_iota(jnp.int32, (MAX_SEL_BLOCKS, tq), 1)
    cur = tt >> 6
    forced = (blk == 0) | (blk == cur) | (blk == cur - 1)
    valid = blk <= cur
    key = jnp.where(forced, FORCED, jnp.where(valid, imp, -1.0))
    sel = jnp.zeros((MAX_SEL_BLOCKS, tq), F32)
    for _ in range(SEL_TOPN):
        mx = jnp.max(key, axis=0, keepdims=True)
        first = jnp.min(jnp.where(key == mx, blk, MAX_SEL_BLOCKS), axis=0, keepdims=True)
        hit = blk == first
        sel = jnp.where(hit, 1.0, sel)
        key = jnp.where(hit, -2.0, key)
    bias_t = jnp.where((sel > 0.5) & (blk < cur), 0.0, NEG)
    bias_ref[0, 0] = jnp.transpose(bias_t).astype(BF16)


def _cmpsel(q_n, kc, vc, wimp_t):
    B, T, _ = q_n.shape
    G = N_KV_GROUPS
    nc = kc.shape[2]
    tq = Q_TILE
    gw = HEADS_PER_GROUP * HEAD_DIM
    return pl.pallas_call(
        _cmpsel_kernel,
        grid=(B, G, T // tq),
        in_specs=[pl.BlockSpec((1, tq, gw), lambda b, g, i: (b, i, g)),
                  pl.BlockSpec((1, 1, nc, LANES), lambda b, g, i: (b, g, 0, 0)),
                  pl.BlockSpec((1, 1, nc, LANES), lambda b, g, i: (b, g, 0, 0)),
                  pl.BlockSpec((MAX_SEL_BLOCKS, nc), lambda b, g, i: (0, 0))],
        out_specs=[pl.BlockSpec((1, tq, gw), lambda b, g, i: (b, i, g)),
                   pl.BlockSpec((1, 1, tq, LANES), lambda b, g, i: (b, g, i, 0))],
        out_shape=[jax.ShapeDtypeStruct((B, T, ATTN_WIDTH), F32),
                   jax.ShapeDtypeStruct((B, G, T, LANES), BF16)],
        compiler_params=_cparams(("parallel", "parallel", "parallel")),
        name="cmpsel",
    )(q_n, kc, vc, wimp_t)


def _attn_kernel(qr_ref, bias_ref, ks_ref, vs_ref, kw_ref, vw_ref, ocmp_ref, gt_ref, sza_ref, out_ref,
                 qa_ref, m_ref, acc_ref, sa_ref, sb_ref):
    tq = qr_ref.shape[1]
    hg = HEADS_PER_GROUP
    rows = hg * tq
    tk = K_TILE
    t0 = pl.multiple_of(pl.program_id(2) * tq, tq)
    q = qr_ref[0]
    bias = bias_ref[0, 0]
    zpad = jnp.zeros((tq, LANES - HEAD_DIM), BF16)
    for h in range(hg):
        qa_ref[h * tq:(h + 1) * tq, 0:LANES] = bias
        qa_ref[h * tq:(h + 1) * tq, LANES:2 * LANES] = jnp.concatenate(
            [q[:, h * HEAD_DIM:(h + 1) * HEAD_DIM], zpad], axis=1)
    qs = qa_ref[:, LANES:LANES + HEAD_DIM]

    def qk(kt, dst_ref):
        k0 = pl.multiple_of(kt * tk, tk)
        dst_ref[...] = _dot_nt(qa_ref[...], ks_ref[0, 0, pl.ds(k0, tk), :])

    def process(src_ref, kt):
        k0 = pl.multiple_of(kt * tk, tk)
        s = src_ref[...]
        m_prev = m_ref[...]
        m_new = jnp.maximum(m_prev, jnp.max(s, axis=-1, keepdims=True))
        alpha = jnp.exp(m_prev - m_new)
        p = jnp.exp(s - m_new[:, 0:1])
        acc_ref[...] = acc_ref[...] * alpha + _dot(p.astype(BF16), vs_ref[0, 0, pl.ds(k0, tk), :])
        m_ref[...] = m_new

    n = t0 // tk + 1
    last = ks_ref.shape[2] // tk - 1
    m_ref[...] = jnp.full(m_ref.shape, NEG, F32)
    acc_ref[...] = jnp.zeros(acc_ref.shape, F32)
    qk(0, sa_ref)

    def body(j, carry):
        k0 = 2 * j
        qk(k0 + 1, sb_ref)
        process(sa_ref, k0)
        qk(jnp.minimum(k0 + 2, last), sa_ref)
        process(sb_ref, k0 + 1)
        return carry

    lax.fori_loop(0, (n + 1) // 2, body, 0)

    r = lax.broadcasted_iota(jnp.int32, (tq, tq), 0)
    c = lax.broadcasted_iota(jnp.int32, (tq, tq), 1)
    zero = jnp.zeros((tq, tq), F32)
    heads = lambda slab: jnp.concatenate([slab] * hg, axis=0)

    own = jnp.where(((c >> 6) == (r >> 6)) & (c <= r), zero, NEG)
    s = _dot_nt(qs, ks_ref[0, 0, pl.ds(t0, tq), LANES:LANES + HEAD_DIM]) + heads(own)
    m_prev = m_ref[...]
    m_new = jnp.maximum(m_prev, jnp.max(s, axis=-1, keepdims=True))
    acc = acc_ref[...] * jnp.exp(m_prev - m_new) + _dot(jnp.exp(s - m_new[:, 0:1]).astype(BF16),
                                                       vs_ref[0, 0, pl.ds(t0, tq), :])
    o_slc = acc[:, 0:HEAD_DIM] / acc[:, HEAD_DIM:HEAD_DIM + 1]

    wk = WINDOW + tq
    ws = pl.multiple_of(jnp.maximum(t0 - WINDOW, 0), tq)
    tri_lo = jnp.where(c > r, zero, NEG)
    tri_hi = jnp.where(c <= r, zero, NEG)
    nchunk = wk // tq
    qi = t0 // tq
    full = t0 >= WINDOW
    band = []
    for a in range(nchunk):
        steady = tri_lo if a == 0 else (tri_hi if a == nchunk - 1 else zero)
        clipped = jnp.where(a < qi, zero, jnp.where(a == qi, tri_hi, NEG))
        band.append(jnp.where(full, steady, clipped))
    s = _dot_nt(qs, kw_ref[0, 0, pl.ds(ws, wk), 0:HEAD_DIM]) + heads(jnp.concatenate(band, axis=1))
    p = jnp.exp(s - jnp.max(s, axis=-1, keepdims=True))
    ow = _dot(p.astype(BF16), vw_ref[0, 0, pl.ds(ws, wk), :])
    o_win = ow[:, 0:HEAD_DIM] / ow[:, HEAD_DIM:HEAD_DIM + 1]

    gt = gt_ref[0, 0]
    oc = ocmp_ref[0]
    parts = []
    for h in range(hg):
        g0 = gt[:, N_BRANCH * h:N_BRANCH * h + 1]
        g1 = gt[:, N_BRANCH * h + 1:N_BRANCH * h + 2]
        g2 = gt[:, N_BRANCH * h + 2:N_BRANCH * h + 3]
        parts.append(g0 * oc[:, h * HEAD_DIM:(h + 1) * HEAD_DIM]
                     + g1 * o_slc[h * tq:(h + 1) * tq] + g2 * o_win[h * tq:(h + 1) * tq])
    attn = jnp.concatenate(parts, axis=1) * sza_ref[0].astype(F32)
    out_ref[0] = attn.astype(BF16)


def _attn(q_r, bias, ks_aug, vs_aug, kw, vw_aug, o_cmp, gates, sza):
    B, T, _ = q_r.shape
    G = N_KV_GROUPS
    tq = Q_TILE
    gw = HEADS_PER_GROUP * HEAD_DIM
    rows = HEADS_PER_GROUP * tq
    tokg = lambda: pl.BlockSpec((1, tq, gw), lambda b, g, i: (b, i, g))
    grp = lambda w: pl.BlockSpec((1, 1, tq, w), lambda b, g, i: (b, g, i, 0))
    res = lambda w: pl.BlockSpec((1, 1, T, w), lambda b, g, i: (b, g, 0, 0))
    return pl.pallas_call(
        _attn_kernel,
        grid=(B, G, T // tq),
        in_specs=[tokg(), grp(LANES), res(2 * LANES), res(LANES), res(LANES), res(LANES),
                  tokg(), grp(LANES), tokg()],
        out_specs=tokg(),
        out_shape=jax.ShapeDtypeStruct((B, T, ATTN_WIDTH), BF16),
        scratch_shapes=[pltpu.VMEM((rows, 2 * LANES), BF16),
                        pltpu.VMEM((rows, LANES), F32),
                        pltpu.VMEM((rows, LANES), F32),
                        pltpu.VMEM((rows, K_TILE), F32),
                        pltpu.VMEM((rows, K_TILE), F32)],
        compiler_params=_cparams(("parallel", "parallel", "arbitrary")),
        name="attn",
    )(q_r, bias, ks_aug, vs_aug, kw, vw_aug, o_cmp, gates, sza)


def _block_transpose8(xs):
    lane = lax.broadcasted_iota(jnp.int32, xs[0].shape, 1)
    xs = list(xs)
    for d in (4, 2, 1):
        w = d * SSM_GROUP
        keep = (lane & w) == 0
        for i in range(8):
            if i & d:
                continue
            lo, hi = xs[i], xs[i + d]
            xs[i] = jnp.where(keep, lo, pltpu.roll(hi, w, 1))
            xs[i + d] = jnp.where(keep, pltpu.roll(lo, LANES - w, 1), hi)
    return xs


def _s5_fold_kernel(u_ref, rh_ref, rl_ref, ug_ref, z_ref, uf_ref):
    nch = z_ref.shape[0]
    gw = S5_STEP * SSM_GROUP
    sw = 2 * SSM_STATE
    uf_ref[...] = u_ref[0].astype(F32)
    cols = [uf_ref[pl.ds(s, nch, stride=S5_STEP), :] for s in range(S5_STEP)]
    halves = [_block_transpose8(cols[0:8]), _block_transpose8(cols[8:16])]
    for g in range(LANES // SSM_GROUP):
        ug = jnp.concatenate([halves[0][g], halves[1][g]], axis=1).astype(BF16)
        ug_ref[:, g * gw:(g + 1) * gw] = ug
        z_ref[:, g * sw:(g + 1) * sw] = _dot(ug, rh_ref[g]) + _dot(ug, rl_ref[g])


SCAN_WAYS = 4


def _s5_scan_kernel(*refs, batch):
    z_refs, (a1_ref, a2_ref), s_refs = refs[:SCAN_WAYS], refs[SCAN_WAYS:SCAN_WAYS + 2], refs[SCAN_WAYS + 2:]
    nch = z_refs[0].shape[0] // batch
    a1 = [a1_ref[:, i * LANES:(i + 1) * LANES] for i in range(SCAN_WAYS)]
    a2 = [a2_ref[:, i * LANES:(i + 1) * LANES] for i in range(SCAN_WAYS)]

    def step(c, carry):
        out = []
        for i in range(SCAN_WAYS):
            s, t = carry[i]
            s_refs[i][pl.ds(c, batch, stride=nch), :] = s
            z = z_refs[i][pl.ds(c, batch, stride=nch), :]
            out.append((a1[i] * s + a2[i] * t + z, a1[i] * t - a2[i] * s + pltpu.roll(z, SSM_STATE, 1)))
        return tuple(out)

    zero = jnp.zeros((batch, LANES), F32)
    lax.fori_loop(0, nch, step, tuple((zero, zero) for _ in range(SCAN_WAYS)), unroll=8)


def _s5_out_kernel(ug_ref, *refs):
    s_refs, (mh_ref, ml_ref, oh_ref, ol_ref, y_ref) = refs[:SCAN_WAYS], refs[SCAN_WAYS:]
    nch = ug_ref.shape[0]
    gw = S5_STEP * SSM_GROUP
    sw = 2 * SSM_STATE
    ys = []
    for g in range(LANES // SSM_GROUP):
        ug = ug_ref[:, g * gw:(g + 1) * gw]
        way_col = g // SCAN_WAYS
        sh, sl = _split_bf16(s_refs[g % SCAN_WAYS][:, way_col * sw:(way_col + 1) * sw])
        y = _dot(ug, mh_ref[g]) + _dot(ug, ml_ref[g])
        ys.append(y + (_dot(sh, oh_ref[g]) + (_dot(sl, oh_ref[g]) + _dot(sh, ol_ref[g]))))
    for k in range(2):
        cols = _block_transpose8([y[:, k * LANES:(k + 1) * LANES] for y in ys])
        for t8 in range(8):
            y_ref[0, pl.ds(8 * k + t8, nch, stride=S5_STEP), :] = cols[t8]


def _s5(u, r_hi, r_lo, m_hi, m_lo, o_hi, o_lo, a1, a2):
    B, T, _ = u.shape
    ng = SSM_GROUPS
    nch = T // S5_STEP
    gw = S5_STEP * SSM_GROUP
    sw = 2 * SSM_STATE
    gpl = LANES // SSM_GROUP
    nslab = ng // gpl
    ops = lambda r, c: pl.BlockSpec((gpl, r, c), lambda b, j: (j, 0, 0))
    ug, z = pl.pallas_call(
        _s5_fold_kernel,
        grid=(B, nslab),
        in_specs=[pl.BlockSpec((1, T, LANES), lambda b, j: (b, 0, j)), ops(gw, sw), ops(gw, sw)],
        out_specs=[pl.BlockSpec((nch, gpl * gw), lambda b, j: (b, j)),
                   pl.BlockSpec((nch, gpl * sw), lambda b, j: (b, j))],
        out_shape=[jax.ShapeDtypeStruct((B * nch, ng * gw), BF16),
                   jax.ShapeDtypeStruct((B * nch, ng * sw), F32)],
        scratch_shapes=[pltpu.VMEM((T, LANES), F32)],
        compiler_params=_cparams(("parallel", "parallel")),
        name="s5_fold",
    )(u, r_hi, r_lo)
    ways = SCAN_WAYS
    col = lambda i: pl.BlockSpec((B * nch, sw), lambda j: (0, ways * j + i))
    coef = pl.BlockSpec((1, ways * sw), lambda j: (0, j))
    s_ways = pl.pallas_call(
        functools.partial(_s5_scan_kernel, batch=B),
        grid=(ng // ways,),
        in_specs=[col(i) for i in range(ways)] + [coef, coef],
        out_specs=[pl.BlockSpec((B * nch, sw), lambda j: (0, j)) for _ in range(ways)],
        out_shape=[jax.ShapeDtypeStruct((B * nch, ng // ways * sw), F32) for _ in range(ways)],
        compiler_params=_cparams(("parallel",)),
        name="s5_scan",
    )(*([z] * ways), a1, a2)
    per_way = gpl // ways
    return pl.pallas_call(
        _s5_out_kernel,
        grid=(B, nslab),
        in_specs=[pl.BlockSpec((nch, gpl * gw), lambda b, j: (b, j))]
                 + [pl.BlockSpec((nch, per_way * sw), lambda b, j: (b, j)) for _ in range(ways)]
                 + [ops(gw, gw), ops(gw, gw), ops(sw, gw), ops(sw, gw)],
        out_specs=pl.BlockSpec((1, T, LANES), lambda b, j: (b, 0, j)),
        out_shape=jax.ShapeDtypeStruct((B, T, ng * SSM_GROUP), F32),
        compiler_params=_cparams(("parallel", "parallel")),
        name="s5_out",
    )(ug, *s_ways, m_hi, m_lo, o_hi, o_lo)


def _cmul(ar, ai, br, bi):
    return ar * br - ai * bi, ar * bi + ai * br


def _s5_operators(a_re, a_im, log_dt, b_re, b_im, c_re, c_im):
    hp = lax.Precision.HIGHEST
    L = S5_STEP
    dt = jnp.exp(log_dt)[:, None]
    mag = jnp.exp(a_re * dt)
    lr, li = mag * jnp.cos(a_im * dt), mag * jnp.sin(a_im * dt)
    den = a_re * a_re + a_im * a_im
    fr, fi = _cmul(lr - 1.0, li, a_re / den, -a_im / den)
    bbr, bbi = _cmul(fr[..., None], fi[..., None], b_re, b_im)
    pr, pi = [jnp.ones_like(lr)], [jnp.zeros_like(li)]
    for _ in range(L):
        nr, ni = _cmul(pr[-1], pi[-1], lr, li)
        pr.append(nr)
        pi.append(ni)
    pr, pi = jnp.stack(pr), jnp.stack(pi)
    cpr, cpi = _cmul(c_re[None], c_im[None], pr[:L, :, None, :], pi[:L, :, None, :])
    kd = (jnp.einsum('dgcp,gpe->dgce', cpr, bbr, precision=hp)
          - jnp.einsum('dgcp,gpe->dgce', cpi, bbi, precision=hp))
    lag = jnp.arange(L)[None, :] - jnp.arange(L)[:, None]
    km = jnp.where((lag >= 0)[:, :, None, None, None], kd[jnp.clip(lag, 0, L - 1)], 0.0)
    ng, cg = b_re.shape[0], b_re.shape[2]
    m_op = km.transpose(2, 0, 4, 1, 3).reshape(ng, L * cg, L * cg)
    rr, ri = _cmul(pr[L - 1 - jnp.arange(L)][..., None], pi[L - 1 - jnp.arange(L)][..., None],
                   bbr[None], bbi[None])
    r_op = jnp.concatenate([rr, ri], axis=2).transpose(1, 0, 3, 2).reshape(ng, L * cg, 2 * SSM_STATE)
    orr, oii = _cmul(c_re[None], c_im[None], pr[1:, :, None, :], pi[1:, :, None, :])
    o_op = jnp.concatenate([orr, -oii], axis=3).transpose(1, 3, 0, 2).reshape(ng, 2 * SSM_STATE, L * cg)
    a1 = jnp.concatenate([pr[L], pr[L]], axis=1).reshape(1, -1)
    a2 = jnp.concatenate([-pi[L], pi[L]], axis=1).reshape(1, -1)
    return m_op, r_op, o_op, a1, a2


def _final_kernel(attn_ref, y_ref, u_ref, szs_ref, x_ref, mod_ref, d_ref, gw_ref, gb_ref, wo_ref, o_ref):
    yv = y_ref[0] + d_ref[...] * u_ref[0].astype(F32)
    yg = jax.nn.gelu(yv)
    gl = jax.nn.sigmoid(_dot(yg.astype(BF16), gw_ref[...]) + gb_ref[...])
    ssm = (yg * gl) * szs_ref[0].astype(F32)
    mix = _dot(attn_ref[0], wo_ref[0:ATTN_WIDTH, :]) + _dot(ssm.astype(BF16), wo_ref[ATTN_WIDTH:D_MODEL, :])
    o_ref[0] = x_ref[0] + mod_ref[0, 2:3, :] * mix


def _final(attn, y, u, szs, x, mod3, d_skip, glu_w, glu_b, w_out):
    B, T, D = x.shape
    tm = ROW_TILE
    tok = lambda w: pl.BlockSpec((1, tm, w), lambda b, i: (b, i, 0))
    const = lambda shape: pl.BlockSpec(shape, lambda b, i: tuple(0 for _ in shape))
    return pl.pallas_call(
        _final_kernel,
        grid=(B, T // tm),
        in_specs=[tok(ATTN_WIDTH), tok(SSM_WIDTH), tok(SSM_WIDTH), tok(SSM_WIDTH), tok(D),
                  pl.BlockSpec((1, 3, D), lambda b, i: (b, 0, 0)),
                  const((1, SSM_WIDTH)), const((SSM_WIDTH, SSM_WIDTH)), const((1, SSM_WIDTH)), const((D, D))],
        out_specs=tok(D),
        out_shape=jax.ShapeDtypeStruct((B, T, D), F32),
        compiler_params=_cparams(("parallel", "parallel")),
        name="final",
    )(attn, y, u, szs, x, mod3, d_skip, glu_w, glu_b, w_out)


def _rope_tables(T):
    inv = 1.0 / (ROPE_THETA ** (jnp.arange(0, HEAD_DIM, 2, dtype=F32) / HEAD_DIM))
    ang = jnp.arange(T, dtype=F32)[:, None] * inv[None, :]
    cos, sin = jnp.cos(ang), jnp.sin(ang)
    reps = LANES // HEAD_DIM
    return jnp.tile(jnp.concatenate([cos, cos], axis=1), (1, reps)), jnp.tile(jnp.concatenate([-sin, sin], axis=1), (1, reps))


def _importance_weights(nc):
    r = SEL_BLOCK // CMP_STRIDE
    ov = CMP_LEN // CMP_STRIDE
    w = np.zeros((MAX_SEL_BLOCKS, nc), np.float32)
    for j in range(MAX_SEL_BLOCKS):
        for m in range(r):
            for n in range(ov):
                i = r * j + m - n
                if 0 <= i < nc:
                    w[j, i] += 1.0
    return jnp.asarray(w, BF16)


def _layer(x, c, w_ada, b_ada, norm_g, w_in, q_norm_g, k_cmp_norm_g, k_slc_norm_g, k_win_norm_g,
           cmp_pos_k, cmp_pos_v, cmp_w1_k, cmp_w2_k, cmp_w1_v, cmp_w2_v,
           ssm_a_re, ssm_a_im, ssm_log_dt, ssm_b_re, ssm_b_im, ssm_c_re, ssm_c_im, ssm_d,
           glu_w, glu_b, w_out):
    B, T, D = x.shape
    G = N_KV_GROUPS
    assert D == D_MODEL and T % K_TILE == 0 and T >= WINDOW + Q_TILE and T // SEL_BLOCK <= MAX_SEL_BLOCKS
    assert T // SEL_BLOCK >= SEL_TOPN

    mod3 = _adaln(c, w_ada, b_ada).reshape(B, 3, D)

    o_gbr = ATTN_WIDTH + 6 * KV_WIDTH + ATTN_WIDTH
    o_u = o_gbr + N_BRANCH * N_HEADS
    npg = N_BRANCH * HEADS_PER_GROUP
    gate_cols = [jnp.pad(w_in[:, o_gbr + g * npg:o_gbr + (g + 1) * npg], ((0, 0), (0, LANES - npg))) for g in range(G)]
    w_pad = jnp.concatenate([w_in[:, :o_gbr], w_in[:, o_u:]] + gate_cols, axis=1).astype(BF16)
    assert w_pad.shape[1] == IN_PAD

    cos_t, sin_t = _rope_tables(T)
    hh = np.arange(LANES) // HEAD_DIM
    ones2 = jnp.asarray(hh[:, None] == hh[None, :], BF16)
    tile2 = lambda g: jnp.tile(g, LANES // HEAD_DIM).reshape(1, LANES)
    (q_n, q_r, kcr, vcr, ks_aug, vs_aug, kw, vw_aug, sza, u, szs, gates) = _inproj(
        x, mod3, norm_g.reshape(1, D), w_pad, cos_t, sin_t, ones2,
        tile2(q_norm_g), tile2(k_slc_norm_g), tile2(k_win_norm_g))

    ns = T // CMP_STRIDE
    half = CMP_STRIDE * HEAD_DIM

    def seg_weight(w1_half):
        w4 = w1_half.reshape(CMP_STRIDE, 1, HEAD_DIM, 1, CMP_HIDDEN)
        eye = jnp.eye(G, dtype=F32).reshape(1, G, 1, G, 1)
        return (w4 * eye).reshape(CMP_STRIDE * G * HEAD_DIM, G * CMP_HIDDEN).astype(BF16)

    def out_weight(w2):
        eye = jnp.eye(G, dtype=F32).reshape(G, 1, G, 1)
        return (w2.reshape(1, CMP_HIDDEN, 1, HEAD_DIM) * eye).reshape(G * CMP_HIDDEN, G * HEAD_DIM).astype(BF16)

    pos_rows = lambda p: jnp.broadcast_to(p.reshape(1, CMP_LEN * HEAD_DIM), (8, CMP_LEN * HEAD_DIM))
    kc, vc = _compress(
        kcr.reshape(B, ns, CMP_STRIDE * KV_WIDTH), vcr.reshape(B, ns, CMP_STRIDE * KV_WIDTH),
        seg_weight(cmp_w1_k[:half]), seg_weight(cmp_w1_k[half:]),
        seg_weight(cmp_w1_v[:half]), seg_weight(cmp_w1_v[half:]),
        cmp_w1_k, cmp_w1_v, pos_rows(cmp_pos_k), pos_rows(cmp_pos_v),
        out_weight(cmp_w2_k), out_weight(cmp_w2_v), tile2(k_cmp_norm_g), ones2)

    o_cmp, bias = _cmpsel(q_n, kc, vc, _importance_weights(ns))
    attn = _attn(q_r, bias, ks_aug, vs_aug, kw, vw_aug, o_cmp, gates, sza)

    m_op, r_op, o_op, a1, a2 = _s5_operators(ssm_a_re, ssm_a_im, ssm_log_dt, ssm_b_re, ssm_b_im, ssm_c_re, ssm_c_im)
    y = _s5(u, *_split_bf16(r_op), *_split_bf16(m_op), *_split_bf16(o_op), a1, a2)

    return _final(attn, y, u, szs, x, mod3, ssm_d.reshape(1, SSM_WIDTH), glu_w.astype(BF16),
                  glu_b.reshape(1, SSM_WIDTH), w_out.astype(BF16))


def kernel(x, c, w_ada, b_ada, norm_g, w_in, q_norm_g, k_cmp_norm_g, k_slc_norm_g, k_win_norm_g, cmp_pos_k, cmp_pos_v, cmp_w1_k, cmp_w2_k, cmp_w1_v, cmp_w2_v, ssm_a_re, ssm_a_im, ssm_log_dt, ssm_b_re, ssm_b_im, ssm_c_re, ssm_c_im, ssm_d, glu_w, glu_b, w_out):
    params = (w_ada, b_ada, norm_g, w_in, q_norm_g, k_cmp_norm_g, k_slc_norm_g, k_win_norm_g, cmp_pos_k, cmp_pos_v,
              cmp_w1_k, cmp_w2_k, cmp_w1_v, cmp_w2_v, ssm_a_re, ssm_a_im, ssm_log_dt, ssm_b_re, ssm_b_im,
              ssm_c_re, ssm_c_im, ssm_d, glu_w, glu_b, w_out)
    for l in range(w_ada.shape[0]):
        x = _layer(x, c, *(p[l] for p in params))
    return x
```

```python
import functools
import math

import jax
import jax.numpy as jnp
import numpy as np
from jax import lax
from jax.experimental import pallas as pl
from jax.experimental.pallas import tpu as pltpu

F32 = jnp.float32
BF16 = jnp.bfloat16

D_MODEL = 1024
ATTN_WIDTH = 512
N_HEADS = 8
HEAD_DIM = 64
N_KV_GROUPS = 2
HEADS_PER_GROUP = N_HEADS // N_KV_GROUPS
KV_WIDTH = N_KV_GROUPS * HEAD_DIM
CMP_LEN = 32
CMP_STRIDE = 16
CMP_HIDDEN = 256
SEL_BLOCK = 64
SEL_TOPN = 16
WINDOW = 512
N_BRANCH = 3
ROPE_THETA = 10000.0
SSM_WIDTH = D_MODEL - ATTN_WIDTH
SSM_GROUP = 16
SSM_GROUPS = SSM_WIDTH // SSM_GROUP
SSM_STATE = 64
EPS = 1e-6

LANES = 128
MAX_SEL_BLOCKS = LANES
NEG = -1e30
Q_SCALE = HEAD_DIM ** -0.5 * math.log2(math.e)
VMEM_LIMIT = 56 * 1024 * 1024

S5_STEP = 16
ROW_TILE = 512
Q_TILE = 128
K_TILE = 512
BRANCH_TILES = 16

_C_Q, _C_KC, _C_VC, _C_KS, _C_VS, _C_KW, _C_VW = 0, 512, 640, 768, 896, 1024, 1152
_C_ZA, _C_U, _C_ZS, _C_GT = 1280, 1792, 2304, 2816
IN_PAD = _C_GT + N_KV_GROUPS * LANES


def _cparams(sem):
    return pltpu.CompilerParams(dimension_semantics=sem, vmem_limit_bytes=VMEM_LIMIT)


def _split_bf16(a):
    hi = a.astype(BF16)
    lo = (a - hi.astype(F32)).astype(BF16)
    return hi, lo


def _dot(a, b):
    return jnp.dot(a, b, preferred_element_type=F32)


def _dot_nt(a, b):
    return lax.dot_general(a, b, (((1,), (1,)), ((), ())), preferred_element_type=F32)


def _dot_f32(a, b):
    ah, al = _split_bf16(a)
    bh, bl = _split_bf16(b)
    return _dot(ah, bh) + (_dot(al, bh) + _dot(ah, bl))


def _adaln_kernel(c_ref, w_ref, b_ref, o_ref):
    c = c_ref[...]
    o_ref[...] = _dot_f32(jax.nn.silu(c), w_ref[...]) + b_ref[...]


def _adaln(c, w_ada, b_ada):
    B, D = c.shape
    n = w_ada.shape[1]
    tn = 1024
    return pl.pallas_call(
        _adaln_kernel,
        grid=(n // tn,),
        in_specs=[pl.BlockSpec((B, D), lambda j: (0, 0)),
                  pl.BlockSpec((D, tn), lambda j: (0, j)),
                  pl.BlockSpec((1, tn), lambda j: (0, j))],
        out_specs=pl.BlockSpec((B, tn), lambda j: (0, j)),
        out_shape=jax.ShapeDtypeStruct((B, n), F32),
        compiler_params=_cparams(("arbitrary",)),
        name="adaln",
    )(c, w_ada, b_ada.reshape(1, n))


def _head_norm(v, ones_ref, gvec):
    ss = _dot((v * v).astype(BF16), ones_ref[...])
    return v * lax.rsqrt(ss * (1.0 / HEAD_DIM) + EPS) * gvec


def _inproj_kernel(x_ref, mod_ref, ng_ref, w_ref, cos_ref, sin_ref, ones_ref, gq_ref, gks_ref, gkw_ref,
                   qn_ref, qr_ref, kc_ref, vc_ref, ksa_ref, vsa_ref, kw_ref, vwa_ref,
                   sza_ref, u_ref, szs_ref, gt_ref):
    tm = x_ref.shape[1]
    ti = pl.program_id(1)
    x = x_ref[0]
    ms = jnp.mean(x * x, axis=-1, keepdims=True)
    shift = mod_ref[0, 0:1, :]
    scale = mod_ref[0, 1:2, :]
    h = (x * lax.rsqrt(ms + EPS)) * ng_ref[...] * (1.0 + scale) + shift
    hb = h.astype(BF16)

    def proj(c0, width):
        return _dot(hb, w_ref[:, c0:c0 + width])

    cosv = cos_ref[...]
    sinv = sin_ref[...]
    lane = lax.broadcasted_iota(jnp.int32, (tm, LANES), 1)
    row = lax.broadcasted_iota(jnp.int32, (tm, LANES), 0)
    first_half = (lane & (HEAD_DIM - 1)) < (HEAD_DIM // 2)
    low = lane < HEAD_DIM

    def rope(v):
        sw = jnp.where(first_half, pltpu.roll(v, LANES - HEAD_DIM // 2, 1), pltpu.roll(v, HEAD_DIM // 2, 1))
        return v * cosv + sw * sinv

    def group_part(v, g):
        return v if g == 0 else pltpu.roll(v, HEAD_DIM, 1)

    for c in range(ATTN_WIDTH // LANES):
        qn = _head_norm(proj(_C_Q + c * LANES, LANES), ones_ref, gq_ref[...]) * Q_SCALE
        qn_ref[0, :, c * LANES:(c + 1) * LANES] = qn.astype(BF16)
        qr_ref[0, :, c * LANES:(c + 1) * LANES] = rope(qn).astype(BF16)

    kc_ref[0] = proj(_C_KC, LANES).astype(BF16)
    vc_ref[0] = proj(_C_VC, LANES).astype(BF16)

    blk = (ti * tm + row) >> 6
    onehot = jnp.where(lane == blk, 1.0, 0.0).astype(BF16)
    ones_col = jnp.where(lane == HEAD_DIM, 1.0, 0.0)

    ksl = rope(_head_norm(proj(_C_KS, LANES), ones_ref, gks_ref[...]))
    vsl = proj(_C_VS, LANES)
    kwn = rope(_head_norm(proj(_C_KW, LANES), ones_ref, gkw_ref[...]))
    vwn = proj(_C_VW, LANES)
    for g in range(N_KV_GROUPS):
        ksa_ref[0, g, :, 0:LANES] = onehot
        ksa_ref[0, g, :, LANES:2 * LANES] = jnp.where(low, group_part(ksl, g), 0.0).astype(BF16)
        vsa_ref[0, g] = jnp.where(low, group_part(vsl, g), ones_col).astype(BF16)
        kw_ref[0, g] = jnp.where(low, group_part(kwn, g), 0.0).astype(BF16)
        vwa_ref[0, g] = jnp.where(low, group_part(vwn, g), ones_col).astype(BF16)
        gt_ref[0, g] = jax.nn.sigmoid(proj(_C_GT + g * LANES, LANES))

    sza_ref[0] = jax.nn.silu(proj(_C_ZA, ATTN_WIDTH)).astype(BF16)
    u_ref[0] = proj(_C_U, SSM_WIDTH).astype(BF16)
    szs_ref[0] = jax.nn.silu(proj(_C_ZS, SSM_WIDTH)).astype(BF16)


def _inproj(x, mod3, norm_g, w_pad, cos_t, sin_t, ones2, gq, gks, gkw):
    B, T, D = x.shape
    tm = ROW_TILE
    G = N_KV_GROUPS
    tok = lambda w: pl.BlockSpec((1, tm, w), lambda b, i: (b, i, 0))
    grp = lambda w: pl.BlockSpec((1, G, tm, w), lambda b, i: (b, 0, i, 0))
    const = lambda shape: pl.BlockSpec(shape, lambda b, i: tuple(0 for _ in shape))
    tshape = lambda w, dt: jax.ShapeDtypeStruct((B, T, w), dt)
    gshape = lambda w, dt: jax.ShapeDtypeStruct((B, G, T, w), dt)
    return pl.pallas_call(
        _inproj_kernel,
        grid=(B, T // tm),
        in_specs=[tok(D),
                  pl.BlockSpec((1, 3, D), lambda b, i: (b, 0, 0)),
                  const((1, D)),
                  const((D, IN_PAD)),
                  pl.BlockSpec((tm, LANES), lambda b, i: (i, 0)),
                  pl.BlockSpec((tm, LANES), lambda b, i: (i, 0)),
                  const((LANES, LANES)), const((1, LANES)), const((1, LANES)), const((1, LANES))],
        out_specs=[tok(ATTN_WIDTH), tok(ATTN_WIDTH), tok(LANES), tok(LANES),
                   grp(2 * LANES), grp(LANES), grp(LANES), grp(LANES),
                   tok(ATTN_WIDTH), tok(SSM_WIDTH), tok(SSM_WIDTH), grp(LANES)],
        out_shape=[tshape(ATTN_WIDTH, BF16), tshape(ATTN_WIDTH, BF16), tshape(LANES, BF16), tshape(LANES, BF16),
                   gshape(2 * LANES, BF16), gshape(LANES, BF16), gshape(LANES, BF16), gshape(LANES, BF16),
                   tshape(ATTN_WIDTH, BF16), tshape(SSM_WIDTH, BF16), tshape(SSM_WIDTH, BF16), gshape(LANES, F32)],
        compiler_params=_cparams(("parallel", "parallel")),
        name="inproj",
    )(x, mod3, norm_g, w_pad, cos_t, sin_t, ones2, gq, gks, gkw)


def _compress_kernel(kx_ref, vx_ref, wtk_ref, wbk_ref, wtv_ref, wbv_ref, w1k_ref, w1v_ref, posk_ref, posv_ref,
                     w2k_ref, w2v_ref, gk_ref, ones_ref, kc_ref, vc_ref):
    ns = kx_ref.shape[1]
    lane = lax.broadcasted_iota(jnp.int32, (ns, LANES), 1)
    row = lax.broadcasted_iota(jnp.int32, (ns, LANES), 0)
    low = lane < HEAD_DIM
    live = row < ns - 1

    def mlp(x_ref, wt_ref, wb_ref, w1_ref, pos_ref, w2_ref):
        x = x_ref[0]
        top = _dot(x, wt_ref[...])
        bot = pltpu.roll(_dot(x, wb_ref[...]), ns - 1, 0)
        b1 = _dot_f32(pos_ref[...], w1_ref[...])[0:1]
        hid = top + bot + jnp.concatenate([b1, b1], axis=1)
        return _dot(jax.nn.gelu(hid).astype(BF16), w2_ref[...])

    kc = mlp(kx_ref, wtk_ref, wbk_ref, w1k_ref, posk_ref, w2k_ref)
    kc = _head_norm(kc, ones_ref, gk_ref[...])
    vc = mlp(vx_ref, wtv_ref, wbv_ref, w1v_ref, posv_ref, w2v_ref)
    for g in range(N_KV_GROUPS):
        kg = kc if g == 0 else pltpu.roll(kc, HEAD_DIM, 1)
        vg = vc if g == 0 else pltpu.roll(vc, HEAD_DIM, 1)
        kc_ref[0, g] = jnp.where(low & live, kg, 0.0).astype(BF16)
        vc_ref[0, g] = jnp.where(low & live, vg, 0.0).astype(BF16)


def _compress(kx, vx, wtk, wbk, wtv, wbv, w1k, w1v, posk, posv, w2k, w2v, gk, ones2):
    B, ns, w = kx.shape
    G = N_KV_GROUPS
    const = lambda a: pl.BlockSpec(a.shape, lambda b: tuple(0 for _ in a.shape))
    consts = (wtk, wbk, wtv, wbv, w1k, w1v, posk, posv, w2k, w2v, gk, ones2)
    return pl.pallas_call(
        _compress_kernel,
        grid=(B,),
        in_specs=[pl.BlockSpec((1, ns, w), lambda b: (b, 0, 0)),
                  pl.BlockSpec((1, ns, w), lambda b: (b, 0, 0))] + [const(a) for a in consts],
        out_specs=[pl.BlockSpec((1, G, ns, LANES), lambda b: (b, 0, 0, 0)),
                   pl.BlockSpec((1, G, ns, LANES), lambda b: (b, 0, 0, 0))],
        out_shape=[jax.ShapeDtypeStruct((B, G, ns, LANES), BF16),
                   jax.ShapeDtypeStruct((B, G, ns, LANES), BF16)],
        compiler_params=_cparams(("parallel",)),
        name="compress",
    )(kx, vx, *consts)


def _stack_heads(q):
    return jnp.concatenate([q[:, h * HEAD_DIM:(h + 1) * HEAD_DIM] for h in range(HEADS_PER_GROUP)], axis=0)


def _branches_kernel(qn_ref, qr_ref, kc_ref, vc_ref, wimp_ref, kw_ref, vw_ref, gt_ref, ocw_ref, bias_ref,
                     sca_ref, scb_ref, swa_ref, swb_ref):
    tq = Q_TILE
    hg = HEADS_PER_GROUP
    nq = qn_ref.shape[1] // tq
    nc = kc_ref.shape[2]
    wk = WINDOW + tq
    first_tile = pl.program_id(2) * nq
    heads = lambda slab: jnp.concatenate([slab] * hg, axis=0)

    r = lax.broadcasted_iota(jnp.int32, (tq, tq), 0)
    c = lax.broadcasted_iota(jnp.int32, (tq, tq), 1)
    zero = jnp.zeros((tq, tq), F32)
    tri_lo = jnp.where(c > r, zero, NEG)
    tri_hi = jnp.where(c <= r, zero, NEG)
    cmp_gap = (lax.broadcasted_iota(jnp.int32, (tq, nc), 1) * CMP_STRIDE + (CMP_LEN - 1)
               - lax.broadcasted_iota(jnp.int32, (tq, nc), 0))
    r_col = lax.broadcasted_iota(jnp.int32, (hg * tq, 1), 0) & (tq - 1)
    blk = lax.broadcasted_iota(jnp.int32, (MAX_SEL_BLOCKS, tq), 0)
    lane_t = lax.broadcasted_iota(jnp.int32, (MAX_SEL_BLOCKS, tq), 1)

    def window_start(t0):
        return pl.multiple_of(jnp.maximum(t0 - WINDOW, 0), tq)

    def scores(i, sc_ref, sw_ref):
        off = pl.multiple_of(i * tq, tq)
        t0 = (first_tile + i) * tq
        sc_ref[...] = _dot_nt(_stack_heads(qn_ref[0, pl.ds(off, tq), :]), kc_ref[0, 0, :, 0:HEAD_DIM])
        sw_ref[...] = _dot_nt(_stack_heads(qr_ref[0, pl.ds(off, tq), :]),
                              kw_ref[0, 0, pl.ds(window_start(t0), wk), 0:HEAD_DIM])

    def finish(i, sc_ref, sw_ref):
        off = pl.multiple_of(i * tq, tq)
        t0 = (first_tile + i) * tq

        s = sc_ref[...] + heads(jnp.where(cmp_gap <= t0, 0.0, NEG))
        e = jnp.exp2(s - jnp.max(s, axis=-1, keepdims=True))
        inv = jnp.where(t0 + r_col >= CMP_LEN - 1, 1.0 / jnp.sum(e, axis=-1, keepdims=True), 0.0)
        p = e * inv
        o_cmp = _dot(p.astype(BF16), vc_ref[0, 0])

        ps = p[0:tq]
        for h in range(1, hg):
            ps = ps + p[h * tq:(h + 1) * tq]
        p_hi, p_lo = _split_bf16(ps)
        w = wimp_ref[...]
        imp = _dot_nt(w, p_hi) + _dot_nt(w, p_lo)

        cur = (t0 + lane_t) >> 6
        forced = (blk == 0) | (blk == cur) | (blk == cur - 1)
        key = jnp.where(forced, -2.0, jnp.where(blk <= cur, imp, -1.0))
        sel = jnp.zeros((MAX_SEL_BLOCKS, tq), F32)
        for _ in range(SEL_TOPN - 3):
            mx = jnp.max(key, axis=0, keepdims=True)
            first = jnp.min(jnp.where(key == mx, blk, MAX_SEL_BLOCKS), axis=0, keepdims=True)
            hit = blk == first
            sel = jnp.where(hit, 1.0, sel)
            key = jnp.where(hit, -2.0, key)
        bias_t = jnp.where(((sel > 0.5) | forced) & (blk < cur), 0.0, NEG)
        bias_ref[0, 0, pl.ds(off, tq), :] = jnp.transpose(bias_t).astype(BF16)

        nchunk = wk // tq
        qi = first_tile + i
        full = t0 >= WINDOW
        band = []
        for a in range(nchunk):
            steady = tri_lo if a == 0 else (tri_hi if a == nchunk - 1 else zero)
            clipped = jnp.where(a < qi, zero, jnp.where(a == qi, tri_hi, NEG))
            band.append(jnp.where(full, steady, clipped))
        s = sw_ref[...] + heads(jnp.concatenate(band, axis=1))
        pw = jnp.exp2(s - jnp.max(s, axis=-1, keepdims=True))
        ow = _dot(pw.astype(BF16), vw_ref[0, 0, pl.ds(window_start(t0), wk), :])
        o_win = ow[:, 0:HEAD_DIM] / ow[:, HEAD_DIM:HEAD_DIM + 1]

        gt = gt_ref[0, 0, pl.ds(off, tq), :]
        parts = []
        for h in range(hg):
            g_cmp = gt[:, N_BRANCH * h:N_BRANCH * h + 1]
            g_win = gt[:, N_BRANCH * h + 2:N_BRANCH * h + 3]
            parts.append(g_cmp * o_cmp[h * tq:(h + 1) * tq, 0:HEAD_DIM] + g_win * o_win[h * tq:(h + 1) * tq])
        ocw_ref[0, pl.ds(off, tq), :] = jnp.concatenate(parts, axis=1)

    scores(0, sca_ref, swa_ref)

    def body(j, carry):
        i = 2 * j
        scores(i + 1, scb_ref, swb_ref)
        finish(i, sca_ref, swa_ref)
        scores(jnp.minimum(i + 2, nq - 1), sca_ref, swa_ref)
        finish(i + 1, scb_ref, swb_ref)
        return carry

    lax.fori_loop(0, nq // 2, body, 0)


def _branches(q_n, q_r, kc, vc, wimp_t, kw, vw_aug, gates):
    B, T, _ = q_n.shape
    G = N_KV_GROUPS
    nc = kc.shape[2]
    tq = Q_TILE
    tb = min(T, BRANCH_TILES * tq)
    assert T % tb == 0 and (tb // tq) % 2 == 0
    gw = HEADS_PER_GROUP * HEAD_DIM
    rows = HEADS_PER_GROUP * tq
    tokg = lambda: pl.BlockSpec((1, tb, gw), lambda b, g, i: (b, i, g))
    grp = lambda: pl.BlockSpec((1, 1, tb, LANES), lambda b, g, i: (b, g, i, 0))
    res = lambda n: pl.BlockSpec((1, 1, n, LANES), lambda b, g, i: (b, g, 0, 0))
    return pl.pallas_call(
        _branches_kernel,
        grid=(B, G, T // tb),
        in_specs=[tokg(), tokg(), res(nc), res(nc),
                  pl.BlockSpec((MAX_SEL_BLOCKS, nc), lambda b, g, i: (0, 0)),
                  res(T), res(T), grp()],
        out_specs=[tokg(), grp()],
        out_shape=[jax.ShapeDtypeStruct((B, T, ATTN_WIDTH), F32),
                   jax.ShapeDtypeStruct((B, G, T, LANES), BF16)],
        scratch_shapes=[pltpu.VMEM((rows, nc), F32), pltpu.VMEM((rows, nc), F32),
                        pltpu.VMEM((rows, WINDOW + tq), F32), pltpu.VMEM((rows, WINDOW + tq), F32)],
        compiler_params=_cparams(("parallel", "parallel", "arbitrary")),
        name="branches",
    )(q_n, q_r, kc, vc, wimp_t, kw, vw_aug, gates)


def _attn_kernel(qr_ref, bias_ref, ks_ref, vs_ref, ocw_ref, gt_ref, sza_ref, out_ref,
                 qa_ref, m_ref, acc_ref, sa_ref, sb_ref):
    tq = qr_ref.shape[1]
    hg = HEADS_PER_GROUP
    rows = hg * tq
    tk = K_TILE
    t0 = pl.multiple_of(pl.program_id(2) * tq, tq)
    q = qr_ref[0]
    bias = bias_ref[0, 0]
    zpad = jnp.zeros((tq, LANES - HEAD_DIM), BF16)
    for h in range(hg):
        qa_ref[h * tq:(h + 1) * tq, 0:LANES] = bias
        qa_ref[h * tq:(h + 1) * tq, LANES:2 * LANES] = jnp.concatenate(
            [q[:, h * HEAD_DIM:(h + 1) * HEAD_DIM], zpad], axis=1)

    def qk(kt, dst_ref):
        k0 = pl.multiple_of(kt * tk, tk)
        dst_ref[...] = _dot_nt(qa_ref[...], ks_ref[0, 0, pl.ds(k0, tk), :])

    def process(src_ref, kt):
        k0 = pl.multiple_of(kt * tk, tk)
        s = src_ref[...]
        m_prev = m_ref[...]
        m_new = jnp.maximum(m_prev, jnp.max(s, axis=-1, keepdims=True))
        alpha = jnp.exp2(m_prev - m_new)
        p = jnp.exp2(s - m_new[:, 0:1])
        acc_ref[...] = acc_ref[...] * alpha + _dot(p.astype(BF16), vs_ref[0, 0, pl.ds(k0, tk), :])
        m_ref[...] = m_new

    n = t0 // tk + 1
    last = ks_ref.shape[2] // tk - 1
    m_ref[...] = jnp.full(m_ref.shape, NEG, F32)
    acc_ref[...] = jnp.zeros(acc_ref.shape, F32)
    qk(0, sa_ref)

    def body(j, carry):
        k0 = 2 * j
        qk(k0 + 1, sb_ref)
        process(sa_ref, k0)
        qk(jnp.minimum(k0 + 2, last), sa_ref)
        process(sb_ref, k0 + 1)
        return carry

    lax.fori_loop(0, (n + 1) // 2, body, 0)

    r = lax.broadcasted_iota(jnp.int32, (tq, tq), 0)
    c = lax.broadcasted_iota(jnp.int32, (tq, tq), 1)
    own = jnp.where(((c >> 6) == (r >> 6)) & (c <= r), 0.0, NEG)
    s = _dot_nt(qa_ref[:, LANES:LANES + HEAD_DIM], ks_ref[0, 0, pl.ds(t0, tq), LANES:LANES + HEAD_DIM])
    s = s + jnp.concatenate([own] * hg, axis=0)
    m_prev = m_ref[...]
    m_new = jnp.maximum(m_prev, jnp.max(s, axis=-1, keepdims=True))
    acc = acc_ref[...] * jnp.exp2(m_prev - m_new) + _dot(jnp.exp2(s - m_new[:, 0:1]).astype(BF16),
                                                       vs_ref[0, 0, pl.ds(t0, tq), :])
    o_slc = acc[:, 0:HEAD_DIM] / acc[:, HEAD_DIM:HEAD_DIM + 1]

    gt = gt_ref[0, 0]
    parts = [gt[:, N_BRANCH * h + 1:N_BRANCH * h + 2] * o_slc[h * tq:(h + 1) * tq] for h in range(hg)]
    attn = (ocw_ref[0] + jnp.concatenate(parts, axis=1)) * sza_ref[0].astype(F32)
    out_ref[0] = attn.astype(BF16)


def _attn(q_r, bias, ks_aug, vs_aug, o_cw, gates, sza):
    B, T, _ = q_r.shape
    G = N_KV_GROUPS
    tq = Q_TILE
    gw = HEADS_PER_GROUP * HEAD_DIM
    rows = HEADS_PER_GROUP * tq
    tokg = lambda: pl.BlockSpec((1, tq, gw), lambda b, g, i: (b, i, g))
    grp = lambda w: pl.BlockSpec((1, 1, tq, w), lambda b, g, i: (b, g, i, 0))
    res = lambda w: pl.BlockSpec((1, 1, T, w), lambda b, g, i: (b, g, 0, 0))
    return pl.pallas_call(
        _attn_kernel,
        grid=(B, G, T // tq),
        in_specs=[tokg(), grp(LANES), res(2 * LANES), res(LANES), tokg(), grp(LANES), tokg()],
        out_specs=tokg(),
        out_shape=jax.ShapeDtypeStruct((B, T, ATTN_WIDTH), BF16),
        scratch_shapes=[pltpu.VMEM((rows, 2 * LANES), BF16),
                        pltpu.VMEM((rows, LANES), F32),
                        pltpu.VMEM((rows, LANES), F32),
                        pltpu.VMEM((rows, K_TILE), F32),
                        pltpu.VMEM((rows, K_TILE), F32)],
        compiler_params=_cparams(("parallel", "parallel", "arbitrary")),
        name="attn",
    )(q_r, bias, ks_aug, vs_aug, o_cw, gates, sza)


def _block_transpose8(xs):
    lane = lax.broadcasted_iota(jnp.int32, xs[0].shape, 1)
    xs = list(xs)
    for d in (4, 2, 1):
        w = d * SSM_GROUP
        keep = (lane & w) == 0
        for i in range(8):
            if i & d:
                continue
            lo, hi = xs[i], xs[i + d]
            xs[i] = jnp.where(keep, lo, pltpu.roll(hi, w, 1))
            xs[i + d] = jnp.where(keep, pltpu.roll(lo, LANES - w, 1), hi)
    return xs


def _s5_fold_kernel(u_ref, rh_ref, rl_ref, ug_ref, z_ref, uf_ref):
    nch = z_ref.shape[0]
    gw = S5_STEP * SSM_GROUP
    sw = 2 * SSM_STATE
    uf_ref[...] = u_ref[0].astype(F32)
    cols = [uf_ref[pl.ds(s, nch, stride=S5_STEP), :] for s in range(S5_STEP)]
    halves = [_block_transpose8(cols[0:8]), _block_transpose8(cols[8:16])]
    for g in range(LANES // SSM_GROUP):
        ug = jnp.concatenate([halves[0][g], halves[1][g]], axis=1).astype(BF16)
        ug_ref[:, g * gw:(g + 1) * gw] = ug
        z_ref[:, g * sw:(g + 1) * sw] = _dot(ug, rh_ref[g]) + _dot(ug, rl_ref[g])


SCAN_WAYS = 4


def _s5_scan_kernel(*refs, batch):
    z_refs, (a1_ref, a2_ref), s_refs = refs[:SCAN_WAYS], refs[SCAN_WAYS:SCAN_WAYS + 2], refs[SCAN_WAYS + 2:]
    nch = z_refs[0].shape[0] // batch
    a1 = [a1_ref[:, i * LANES:(i + 1) * LANES] for i in range(SCAN_WAYS)]
    a2 = [a2_ref[:, i * LANES:(i + 1) * LANES] for i in range(SCAN_WAYS)]

    def step(c, carry):
        out = []
        for i in range(SCAN_WAYS):
            s, t = carry[i]
            s_refs[i][pl.ds(c, batch, stride=nch), :] = s
            z = z_refs[i][pl.ds(c, batch, stride=nch), :]
            out.append((a1[i] * s + a2[i] * t + z, a1[i] * t - a2[i] * s + pltpu.roll(z, SSM_STATE, 1)))
        return tuple(out)

    zero = jnp.zeros((batch, LANES), F32)
    lax.fori_loop(0, nch, step, tuple((zero, zero) for _ in range(SCAN_WAYS)), unroll=8)


def _s5_out_kernel(ug_ref, *refs):
    s_refs, (mh_ref, ml_ref, oh_ref, ol_ref, y_ref) = refs[:SCAN_WAYS], refs[SCAN_WAYS:]
    nch = ug_ref.shape[0]
    gw = S5_STEP * SSM_GROUP
    sw = 2 * SSM_STATE
    ys = []
    for g in range(LANES // SSM_GROUP):
        ug = ug_ref[:, g * gw:(g + 1) * gw]
        way_col = g // SCAN_WAYS
        sh, sl = _split_bf16(s_refs[g % SCAN_WAYS][:, way_col * sw:(way_col + 1) * sw])
        y = _dot(ug, mh_ref[g]) + _dot(ug, ml_ref[g])
        ys.append(y + (_dot(sh, oh_ref[g]) + (_dot(sl, oh_ref[g]) + _dot(sh, ol_ref[g]))))
    for k in range(2):
        cols = _block_transpose8([y[:, k * LANES:(k + 1) * LANES] for y in ys])
        for t8 in range(8):
            y_ref[0, pl.ds(8 * k + t8, nch, stride=S5_STEP), :] = cols[t8]


def _s5(u, r_hi, r_lo, m_hi, m_lo, o_hi, o_lo, a1, a2):
    B, T, _ = u.shape
    ng = SSM_GROUPS
    nch = T // S5_STEP
    gw = S5_STEP * SSM_GROUP
    sw = 2 * SSM_STATE
    gpl = LANES // SSM_GROUP
    nslab = ng // gpl
    ops = lambda r, c: pl.BlockSpec((gpl, r, c), lambda b, j: (j, 0, 0))
    ug, z = pl.pallas_call(
        _s5_fold_kernel,
        grid=(B, nslab),
        in_specs=[pl.BlockSpec((1, T, LANES), lambda b, j: (b, 0, j)), ops(gw, sw), ops(gw, sw)],
        out_specs=[pl.BlockSpec((nch, gpl * gw), lambda b, j: (b, j)),
                   pl.BlockSpec((nch, gpl * sw), lambda b, j: (b, j))],
        out_shape=[jax.ShapeDtypeStruct((B * nch, ng * gw), BF16),
                   jax.ShapeDtypeStruct((B * nch, ng * sw), F32)],
        scratch_shapes=[pltpu.VMEM((T, LANES), F32)],
        compiler_params=_cparams(("parallel", "parallel")),
        name="s5_fold",
    )(u, r_hi, r_lo)
    ways = SCAN_WAYS
    col = lambda i: pl.BlockSpec((B * nch, sw), lambda j: (0, ways * j + i))
    coef = pl.BlockSpec((1, ways * sw), lambda j: (0, j))
    s_ways = pl.pallas_call(
        functools.partial(_s5_scan_kernel, batch=B),
        grid=(ng // ways,),
        in_specs=[col(i) for i in range(ways)] + [coef, coef],
        out_specs=[pl.BlockSpec((B * nch, sw), lambda j: (0, j)) for _ in range(ways)],
        out_shape=[jax.ShapeDtypeStruct((B * nch, ng // ways * sw), F32) for _ in range(ways)],
        compiler_params=_cparams(("parallel",)),
        name="s5_scan",
    )(*([z] * ways), a1, a2)
    per_way = gpl // ways
    return pl.pallas_call(
        _s5_out_kernel,
        grid=(B, nslab),
        in_specs=[pl.BlockSpec((nch, gpl * gw), lambda b, j: (b, j))]
                 + [pl.BlockSpec((nch, per_way * sw), lambda b, j: (b, j)) for _ in range(ways)]
                 + [ops(gw, gw), ops(gw, gw), ops(sw, gw), ops(sw, gw)],
        out_specs=pl.BlockSpec((1, T, LANES), lambda b, j: (b, 0, j)),
        out_shape=jax.ShapeDtypeStruct((B, T, ng * SSM_GROUP), F32),
        compiler_params=_cparams(("parallel", "parallel")),
        name="s5_out",
    )(ug, *s_ways, m_hi, m_lo, o_hi, o_lo)


def _cmul(ar, ai, br, bi):
    return ar * br - ai * bi, ar * bi + ai * br


def _s5_operators(a_re, a_im, log_dt, b_re, b_im, c_re, c_im):
    hp = lax.Precision.HIGHEST
    L = S5_STEP
    dt = jnp.exp(log_dt)[:, None]
    mag = jnp.exp(a_re * dt)
    lr, li = mag * jnp.cos(a_im * dt), mag * jnp.sin(a_im * dt)
    den = a_re * a_re + a_im * a_im
    fr, fi = _cmul(lr - 1.0, li, a_re / den, -a_im / den)
    bbr, bbi = _cmul(fr[..., None], fi[..., None], b_re, b_im)
    pr, pi = [jnp.ones_like(lr)], [jnp.zeros_like(li)]
    for _ in range(L):
        nr, ni = _cmul(pr[-1], pi[-1], lr, li)
        pr.append(nr)
        pi.append(ni)
    pr, pi = jnp.stack(pr), jnp.stack(pi)
    cpr, cpi = _cmul(c_re[None], c_im[None], pr[:L, :, None, :], pi[:L, :, None, :])
    kd = (jnp.einsum('dgcp,gpe->dgce', cpr, bbr, precision=hp)
          - jnp.einsum('dgcp,gpe->dgce', cpi, bbi, precision=hp))
    lag = jnp.arange(L)[None, :] - jnp.arange(L)[:, None]
    km = jnp.where((lag >= 0)[:, :, None, None, None], kd[jnp.clip(lag, 0, L - 1)], 0.0)
    ng, cg = b_re.shape[0], b_re.shape[2]
    m_op = km.transpose(2, 0, 4, 1, 3).reshape(ng, L * cg, L * cg)
    rr, ri = _cmul(pr[L - 1 - jnp.arange(L)][..., None], pi[L - 1 - jnp.arange(L)][..., None],
                   bbr[None], bbi[None])
    r_op = jnp.concatenate([rr, ri], axis=2).transpose(1, 0, 3, 2).reshape(ng, L * cg, 2 * SSM_STATE)
    orr, oii = _cmul(c_re[None], c_im[None], pr[1:, :, None, :], pi[1:, :, None, :])
    o_op = jnp.concatenate([orr, -oii], axis=3).transpose(1, 3, 0, 2).reshape(ng, 2 * SSM_STATE, L * cg)
    a1 = jnp.concatenate([pr[L], pr[L]], axis=1).reshape(1, -1)
    a2 = jnp.concatenate([-pi[L], pi[L]], axis=1).reshape(1, -1)
    return m_op, r_op, o_op, a1, a2


def _final_kernel(attn_ref, y_ref, u_ref, szs_ref, x_ref, mod_ref, d_ref, gw_ref, gb_ref, wo_ref, o_ref):
    yv = y_ref[0] + d_ref[...] * u_ref[0].astype(F32)
    yg = jax.nn.gelu(yv)
    gl = jax.nn.sigmoid(_dot(yg.astype(BF16), gw_ref[...]) + gb_ref[...])
    ssm = (yg * gl) * szs_ref[0].astype(F32)
    mix = _dot(attn_ref[0], wo_ref[0:ATTN_WIDTH, :]) + _dot(ssm.astype(BF16), wo_ref[ATTN_WIDTH:D_MODEL, :])
    o_ref[0] = x_ref[0] + mod_ref[0, 2:3, :] * mix


def _final(attn, y, u, szs, x, mod3, d_skip, glu_w, glu_b, w_out):
    B, T, D = x.shape
    tm = ROW_TILE
    tok = lambda w: pl.BlockSpec((1, tm, w), lambda b, i: (b, i, 0))
    const = lambda shape: pl.BlockSpec(shape, lambda b, i: tuple(0 for _ in shape))
    return pl.pallas_call(
        _final_kernel,
        grid=(B, T // tm),
        in_specs=[tok(ATTN_WIDTH), tok(SSM_WIDTH), tok(SSM_WIDTH), tok(SSM_WIDTH), tok(D),
                  pl.BlockSpec((1, 3, D), lambda b, i: (b, 0, 0)),
                  const((1, SSM_WIDTH)), const((SSM_WIDTH, SSM_WIDTH)), const((1, SSM_WIDTH)), const((D, D))],
        out_specs=tok(D),
        out_shape=jax.ShapeDtypeStruct((B, T, D), F32),
        compiler_params=_cparams(("parallel", "parallel")),
        name="final",
    )(attn, y, u, szs, x, mod3, d_skip, glu_w, glu_b, w_out)


def _rope_tables(T):
    inv = 1.0 / (ROPE_THETA ** (jnp.arange(0, HEAD_DIM, 2, dtype=F32) / HEAD_DIM))
    ang = jnp.arange(T, dtype=F32)[:, None] * inv[None, :]
    cos, sin = jnp.cos(ang), jnp.sin(ang)
    reps = LANES // HEAD_DIM
    return jnp.tile(jnp.concatenate([cos, cos], axis=1), (1, reps)), jnp.tile(jnp.concatenate([-sin, sin], axis=1), (1, reps))


def _importance_weights(nc):
    r = SEL_BLOCK // CMP_STRIDE
    ov = CMP_LEN // CMP_STRIDE
    w = np.zeros((MAX_SEL_BLOCKS, nc), np.float32)
    for j in range(MAX_SEL_BLOCKS):
        for m in range(r):
            for n in range(ov):
                i = r * j + m - n
                if 0 <= i < nc:
                    w[j, i] += 1.0
    return jnp.asarray(w, BF16)


def _layer(x, c, w_ada, b_ada, norm_g, w_in, q_norm_g, k_cmp_norm_g, k_slc_norm_g, k_win_norm_g,
           cmp_pos_k, cmp_pos_v, cmp_w1_k, cmp_w2_k, cmp_w1_v, cmp_w2_v,
           ssm_a_re, ssm_a_im, ssm_log_dt, ssm_b_re, ssm_b_im, ssm_c_re, ssm_c_im, ssm_d,
           glu_w, glu_b, w_out):
    B, T, D = x.shape
    G = N_KV_GROUPS
    assert D == D_MODEL and T % K_TILE == 0 and T >= WINDOW + Q_TILE and T // SEL_BLOCK <= MAX_SEL_BLOCKS
    assert T // SEL_BLOCK >= SEL_TOPN

    mod3 = _adaln(c, w_ada, b_ada).reshape(B, 3, D)

    o_gbr = ATTN_WIDTH + 6 * KV_WIDTH + ATTN_WIDTH
    o_u = o_gbr + N_BRANCH * N_HEADS
    npg = N_BRANCH * HEADS_PER_GROUP
    gate_cols = [jnp.pad(w_in[:, o_gbr + g * npg:o_gbr + (g + 1) * npg], ((0, 0), (0, LANES - npg))) for g in range(G)]
    w_pad = jnp.concatenate([w_in[:, :o_gbr], w_in[:, o_u:]] + gate_cols, axis=1).astype(BF16)
    assert w_pad.shape[1] == IN_PAD

    cos_t, sin_t = _rope_tables(T)
    hh = np.arange(LANES) // HEAD_DIM
    ones2 = jnp.asarray(hh[:, None] == hh[None, :], BF16)
    tile2 = lambda g: jnp.tile(g, LANES // HEAD_DIM).reshape(1, LANES)
    (q_n, q_r, kcr, vcr, ks_aug, vs_aug, kw, vw_aug, sza, u, szs, gates) = _inproj(
        x, mod3, norm_g.reshape(1, D), w_pad, cos_t, sin_t, ones2,
        tile2(q_norm_g), tile2(k_slc_norm_g), tile2(k_win_norm_g))

    ns = T // CMP_STRIDE
    half = CMP_STRIDE * HEAD_DIM

    def seg_weight(w1_half):
        w4 = w1_half.reshape(CMP_STRIDE, 1, HEAD_DIM, 1, CMP_HIDDEN)
        eye = jnp.eye(G, dtype=F32).reshape(1, G, 1, G, 1)
        return (w4 * eye).reshape(CMP_STRIDE * G * HEAD_DIM, G * CMP_HIDDEN).astype(BF16)

    def out_weight(w2):
        eye = jnp.eye(G, dtype=F32).reshape(G, 1, G, 1)
        return (w2.reshape(1, CMP_HIDDEN, 1, HEAD_DIM) * eye).reshape(G * CMP_HIDDEN, G * HEAD_DIM).astype(BF16)

    pos_rows = lambda p: jnp.broadcast_to(p.reshape(1, CMP_LEN * HEAD_DIM), (8, CMP_LEN * HEAD_DIM))
    kc, vc = _compress(
        kcr.reshape(B, ns, CMP_STRIDE * KV_WIDTH), vcr.reshape(B, ns, CMP_STRIDE * KV_WIDTH),
        seg_weight(cmp_w1_k[:half]), seg_weight(cmp_w1_k[half:]),
        seg_weight(cmp_w1_v[:half]), seg_weight(cmp_w1_v[half:]),
        cmp_w1_k, cmp_w1_v, pos_rows(cmp_pos_k), pos_rows(cmp_pos_v),
        out_weight(cmp_w2_k), out_weight(cmp_w2_v), tile2(k_cmp_norm_g), ones2)

    o_cw, bias = _branches(q_n, q_r, kc, vc, _importance_weights(ns), kw, vw_aug, gates)
    attn = _attn(q_r, bias, ks_aug, vs_aug, o_cw, gates, sza)

    m_op, r_op, o_op, a1, a2 = _s5_operators(ssm_a_re, ssm_a_im, ssm_log_dt, ssm_b_re, ssm_b_im, ssm_c_re, ssm_c_im)
    y = _s5(u, *_split_bf16(r_op), *_split_bf16(m_op), *_split_bf16(o_op), a1, a2)

    return _final(attn, y, u, szs, x, mod3, ssm_d.reshape(1, SSM_WIDTH), glu_w.astype(BF16),
                  glu_b.reshape(1, SSM_WIDTH), w_out.astype(BF16))


def kernel(x, c, w_ada, b_ada, norm_g, w_in, q_norm_g, k_cmp_norm_g, k_slc_norm_g, k_win_norm_g, cmp_pos_k, cmp_pos_v, cmp_w1_k, cmp_w2_k, cmp_w1_v, cmp_w2_v, ssm_a_re, ssm_a_im, ssm_log_dt, ssm_b_re, ssm_b_im, ssm_c_re, ssm_c_im, ssm_d, glu_w, glu_b, w_out):
    params = (w_ada, b_ada, norm_g, w_in, q_norm_g, k_cmp_norm_g, k_slc_norm_g, k_win_norm_g, cmp_pos_k, cmp_pos_v,
              cmp_w1_k, cmp_w2_k, cmp_w1_v, cmp_w2_v, ssm_a_re, ssm_a_im, ssm_log_dt, ssm_b_re, ssm_b_im,
              ssm_c_re, ssm_c_im, ssm_d, glu_w, glu_b, w_out)
    for l in range(w_ada.shape[0]):
        x = _layer(x, c, *(p[l] for p in params))
    return x
```

```python
import functools
import math

import jax
import jax.numpy as jnp
import numpy as np
from jax import lax
from jax.experimental import pallas as pl
from jax.experimental.pallas import tpu as pltpu

F32 = jnp.float32
BF16 = jnp.bfloat16

D_MODEL = 1024
ATTN_WIDTH = 512
N_HEADS = 8
HEAD_DIM = 64
N_KV_GROUPS = 2
HEADS_PER_GROUP = N_HEADS // N_KV_GROUPS
KV_WIDTH = N_KV_GROUPS * HEAD_DIM
CMP_LEN = 32
CMP_STRIDE = 16
CMP_HIDDEN = 256
SEL_BLOCK = 64
SEL_TOPN = 16
WINDOW = 512
N_BRANCH = 3
ROPE_THETA = 10000.0
SSM_WIDTH = D_MODEL - ATTN_WIDTH
SSM_GROUP = 16
SSM_GROUPS = SSM_WIDTH // SSM_GROUP
SSM_STATE = 64
EPS = 1e-6

LANES = 128
MAX_SEL_BLOCKS = LANES
NEG = -1e30
Q_SCALE = HEAD_DIM ** -0.5 * math.log2(math.e)
VMEM_LIMIT = 56 * 1024 * 1024

S5_STEP = 16
ROW_TILE = 512
Q_TILE = 128
ATTN_Q_TILE = 256
K_TILE = 512
BRANCH_TILES = 16

_C_Q, _C_KC, _C_VC, _C_KS, _C_VS, _C_KW, _C_VW = 0, 512, 640, 768, 896, 1024, 1152
_C_ZA, _C_U, _C_ZS, _C_GT = 1280, 1792, 2304, 2816
IN_PAD = _C_GT + N_KV_GROUPS * LANES


def _cparams(sem):
    return pltpu.CompilerParams(dimension_semantics=sem, vmem_limit_bytes=VMEM_LIMIT)


def _split_bf16(a):
    hi = a.astype(BF16)
    lo = (a - hi.astype(F32)).astype(BF16)
    return hi, lo


def _dot(a, b):
    return jnp.dot(a, b, preferred_element_type=F32)


def _dot_nt(a, b):
    return lax.dot_general(a, b, (((1,), (1,)), ((), ())), preferred_element_type=F32)


def _dot_f32(a, b):
    ah, al = _split_bf16(a)
    bh, bl = _split_bf16(b)
    return _dot(ah, bh) + (_dot(al, bh) + _dot(ah, bl))


def _adaln_kernel(c_ref, w_ref, b_ref, o_ref):
    c = c_ref[...]
    o_ref[...] = _dot_f32(jax.nn.silu(c), w_ref[...]) + b_ref[...]


def _adaln(c, w_ada, b_ada):
    B, D = c.shape
    n = w_ada.shape[1]
    tn = 1024
    return pl.pallas_call(
        _adaln_kernel,
        grid=(n // tn,),
        in_specs=[pl.BlockSpec((B, D), lambda j: (0, 0)),
                  pl.BlockSpec((D, tn), lambda j: (0, j)),
                  pl.BlockSpec((1, tn), lambda j: (0, j))],
        out_specs=pl.BlockSpec((B, tn), lambda j: (0, j)),
        out_shape=jax.ShapeDtypeStruct((B, n), F32),
        compiler_params=_cparams(("arbitrary",)),
        name="adaln",
    )(c, w_ada, b_ada.reshape(1, n))


def _head_norm(v, ones_ref, gvec):
    ss = _dot((v * v).astype(BF16), ones_ref[...])
    return v * lax.rsqrt(ss * (1.0 / HEAD_DIM) + EPS) * gvec


def _inproj_kernel(x_ref, mod_ref, ng_ref, w_ref, cos_ref, sin_ref, ones_ref, gq_ref, gks_ref, gkw_ref,
                   qn_ref, qr_ref, kc_ref, vc_ref, ksa_ref, vsa_ref, kw_ref, vwa_ref,
                   sza_ref, u_ref, szs_ref, gt_ref):
    tm = x_ref.shape[1]
    ti = pl.program_id(1)
    x = x_ref[0]
    ms = jnp.mean(x * x, axis=-1, keepdims=True)
    shift = mod_ref[0, 0:1, :]
    scale = mod_ref[0, 1:2, :]
    h = (x * lax.rsqrt(ms + EPS)) * ng_ref[...] * (1.0 + scale) + shift
    hb = h.astype(BF16)

    def proj(c0, width):
        return _dot(hb, w_ref[:, c0:c0 + width])

    cosv = cos_ref[...]
    sinv = sin_ref[...]
    lane = lax.broadcasted_iota(jnp.int32, (tm, LANES), 1)
    row = lax.broadcasted_iota(jnp.int32, (tm, LANES), 0)
    first_half = (lane & (HEAD_DIM - 1)) < (HEAD_DIM // 2)
    low = lane < HEAD_DIM

    def rope(v):
        sw = jnp.where(first_half, pltpu.roll(v, LANES - HEAD_DIM // 2, 1), pltpu.roll(v, HEAD_DIM // 2, 1))
        return v * cosv + sw * sinv

    def group_part(v, g):
        return v if g == 0 else pltpu.roll(v, HEAD_DIM, 1)

    def proj_pair(c0):
        pair = proj(c0, 2 * LANES)
        return pair[:, 0:LANES], pair[:, LANES:2 * LANES]

    for c2 in range(ATTN_WIDTH // (2 * LANES)):
        for c, q in zip((2 * c2, 2 * c2 + 1), proj_pair(_C_Q + 2 * c2 * LANES)):
            qn = _head_norm(q, ones_ref, gq_ref[...]) * Q_SCALE
            qn_ref[0, :, c * LANES:(c + 1) * LANES] = qn.astype(BF16)
            qr_ref[0, :, c * LANES:(c + 1) * LANES] = rope(qn).astype(BF16)

    kcr, vcr = proj_pair(_C_KC)
    kc_ref[0] = kcr.astype(BF16)
    vc_ref[0] = vcr.astype(BF16)

    blk = (ti * tm + row) >> 6
    onehot = jnp.where(lane == blk, 1.0, 0.0).astype(BF16)
    ones_col = jnp.where(lane == HEAD_DIM, 1.0, 0.0)

    ksl, vsl = proj_pair(_C_KS)
    kwn, vwn = proj_pair(_C_KW)
    ksl = rope(_head_norm(ksl, ones_ref, gks_ref[...]))
    kwn = rope(_head_norm(kwn, ones_ref, gkw_ref[...]))
    gate_logits = proj_pair(_C_GT)
    for g in range(N_KV_GROUPS):
        ksa_ref[0, g, :, 0:LANES] = onehot
        ksa_ref[0, g, :, LANES:2 * LANES] = jnp.where(low, group_part(ksl, g), 0.0).astype(BF16)
        vsa_ref[0, g] = jnp.where(low, group_part(vsl, g), ones_col).astype(BF16)
        kw_ref[0, g] = jnp.where(low, group_part(kwn, g), 0.0).astype(BF16)
        vwa_ref[0, g] = jnp.where(low, group_part(vwn, g), ones_col).astype(BF16)
        gt_ref[0, g] = jax.nn.sigmoid(gate_logits[g])

    sza_ref[0] = jax.nn.silu(proj(_C_ZA, ATTN_WIDTH)).astype(BF16)
    u_ref[0] = proj(_C_U, SSM_WIDTH).astype(BF16)
    szs_ref[0] = jax.nn.silu(proj(_C_ZS, SSM_WIDTH)).astype(BF16)


def _inproj(x, mod3, norm_g, w_pad, cos_t, sin_t, ones2, gq, gks, gkw):
    B, T, D = x.shape
    tm = ROW_TILE
    G = N_KV_GROUPS
    tok = lambda w: pl.BlockSpec((1, tm, w), lambda b, i: (b, i, 0))
    grp = lambda w: pl.BlockSpec((1, G, tm, w), lambda b, i: (b, 0, i, 0))
    const = lambda shape: pl.BlockSpec(shape, lambda b, i: tuple(0 for _ in shape))
    tshape = lambda w, dt: jax.ShapeDtypeStruct((B, T, w), dt)
    gshape = lambda w, dt: jax.ShapeDtypeStruct((B, G, T, w), dt)
    return pl.pallas_call(
        _inproj_kernel,
        grid=(B, T // tm),
        in_specs=[tok(D),
                  pl.BlockSpec((1, 3, D), lambda b, i: (b, 0, 0)),
                  const((1, D)),
                  const((D, IN_PAD)),
                  pl.BlockSpec((tm, LANES), lambda b, i: (i, 0)),
                  pl.BlockSpec((tm, LANES), lambda b, i: (i, 0)),
                  const((LANES, LANES)), const((1, LANES)), const((1, LANES)), const((1, LANES))],
        out_specs=[tok(ATTN_WIDTH), tok(ATTN_WIDTH), tok(LANES), tok(LANES),
                   grp(2 * LANES), grp(LANES), grp(LANES), grp(LANES),
                   tok(ATTN_WIDTH), tok(SSM_WIDTH), tok(SSM_WIDTH), grp(LANES)],
        out_shape=[tshape(ATTN_WIDTH, BF16), tshape(ATTN_WIDTH, BF16), tshape(LANES, BF16), tshape(LANES, BF16),
                   gshape(2 * LANES, BF16), gshape(LANES, BF16), gshape(LANES, BF16), gshape(LANES, BF16),
                   tshape(ATTN_WIDTH, BF16), tshape(SSM_WIDTH, BF16), tshape(SSM_WIDTH, BF16), gshape(LANES, F32)],
        compiler_params=_cparams(("parallel", "parallel")),
        name="inproj",
    )(x, mod3, norm_g, w_pad, cos_t, sin_t, ones2, gq, gks, gkw)


def _compress_kernel(kx_ref, vx_ref, wtk_ref, wbk_ref, wtv_ref, wbv_ref, w1k_ref, w1v_ref, posk_ref, posv_ref,
                     w2k_ref, w2v_ref, gk_ref, ones_ref, kc_ref, vc_ref):
    ns = kx_ref.shape[1]
    lane = lax.broadcasted_iota(jnp.int32, (ns, LANES), 1)
    row = lax.broadcasted_iota(jnp.int32, (ns, LANES), 0)
    low = lane < HEAD_DIM
    live = row < ns - 1

    def mlp(x_ref, wt_ref, wb_ref, w1_ref, pos_ref, w2_ref):
        x = x_ref[0]
        top = _dot(x, wt_ref[...])
        bot = pltpu.roll(_dot(x, wb_ref[...]), ns - 1, 0)
        b1 = _dot_f32(pos_ref[...], w1_ref[...])[0:1]
        hid = top + bot + jnp.concatenate([b1, b1], axis=1)
        return _dot(jax.nn.gelu(hid).astype(BF16), w2_ref[...])

    kc = mlp(kx_ref, wtk_ref, wbk_ref, w1k_ref, posk_ref, w2k_ref)
    kc = _head_norm(kc, ones_ref, gk_ref[...])
    vc = mlp(vx_ref, wtv_ref, wbv_ref, w1v_ref, posv_ref, w2v_ref)
    for g in range(N_KV_GROUPS):
        kg = kc if g == 0 else pltpu.roll(kc, HEAD_DIM, 1)
        vg = vc if g == 0 else pltpu.roll(vc, HEAD_DIM, 1)
        kc_ref[0, g] = jnp.where(low & live, kg, 0.0).astype(BF16)
        vc_ref[0, g] = jnp.where(low & live, vg, 0.0).astype(BF16)


def _compress(kx, vx, wtk, wbk, wtv, wbv, w1k, w1v, posk, posv, w2k, w2v, gk, ones2):
    B, ns, w = kx.shape
    G = N_KV_GROUPS
    const = lambda a: pl.BlockSpec(a.shape, lambda b: tuple(0 for _ in a.shape))
    consts = (wtk, wbk, wtv, wbv, w1k, w1v, posk, posv, w2k, w2v, gk, ones2)
    return pl.pallas_call(
        _compress_kernel,
        grid=(B,),
        in_specs=[pl.BlockSpec((1, ns, w), lambda b: (b, 0, 0)),
                  pl.BlockSpec((1, ns, w), lambda b: (b, 0, 0))] + [const(a) for a in consts],
        out_specs=[pl.BlockSpec((1, G, ns, LANES), lambda b: (b, 0, 0, 0)),
                   pl.BlockSpec((1, G, ns, LANES), lambda b: (b, 0, 0, 0))],
        out_shape=[jax.ShapeDtypeStruct((B, G, ns, LANES), BF16),
                   jax.ShapeDtypeStruct((B, G, ns, LANES), BF16)],
        compiler_params=_cparams(("parallel",)),
        name="compress",
    )(kx, vx, *consts)


def _stack_heads(q):
    return jnp.concatenate([q[:, h * HEAD_DIM:(h + 1) * HEAD_DIM] for h in range(HEADS_PER_GROUP)], axis=0)


def _branches_kernel(qn_ref, qr_ref, kc_ref, vc_ref, wimp_ref, kw_ref, vw_ref, gt_ref, ocw_ref, bias_ref,
                     sca_ref, scb_ref, swa_ref, swb_ref):
    tq = Q_TILE
    hg = HEADS_PER_GROUP
    nq = qn_ref.shape[1] // tq
    nc = kc_ref.shape[2]
    wk = WINDOW + tq
    first_tile = pl.program_id(2) * nq
    heads = lambda slab: jnp.concatenate([slab] * hg, axis=0)

    def window_start(t0):
        return pl.multiple_of(jnp.maximum(t0 - WINDOW, 0), tq)

    def scores(i, sc_ref, sw_ref):
        off = pl.multiple_of(i * tq, tq)
        t0 = (first_tile + i) * tq
        sc_ref[...] = _dot_nt(_stack_heads(qn_ref[0, pl.ds(off, tq), :]), kc_ref[0, 0, :, 0:HEAD_DIM])
        sw_ref[...] = _dot_nt(_stack_heads(qr_ref[0, pl.ds(off, tq), :]),
                              kw_ref[0, 0, pl.ds(window_start(t0), wk), 0:HEAD_DIM])

    def finish(i, sc_ref, sw_ref):
        off = pl.multiple_of(i * tq, tq)
        t0 = (first_tile + i) * tq
        r = lax.broadcasted_iota(jnp.int32, (tq, tq), 0)
        c = lax.broadcasted_iota(jnp.int32, (tq, tq), 1)
        zero = jnp.zeros((tq, tq), F32)
        tri_lo = jnp.where(c > r, zero, NEG)
        tri_hi = jnp.where(c <= r, zero, NEG)
        cmp_gap = (lax.broadcasted_iota(jnp.int32, (tq, nc), 1) * CMP_STRIDE + (CMP_LEN - 1)
                   - lax.broadcasted_iota(jnp.int32, (tq, nc), 0))
        r_col = lax.broadcasted_iota(jnp.int32, (hg * tq, 1), 0) & (tq - 1)
        blk = lax.broadcasted_iota(jnp.int32, (MAX_SEL_BLOCKS, tq), 0)
        lane_t = lax.broadcasted_iota(jnp.int32, (MAX_SEL_BLOCKS, tq), 1)

        s = sc_ref[...] + heads(jnp.where(cmp_gap <= t0, 0.0, NEG))
        e = jnp.exp2(s - jnp.max(s, axis=-1, keepdims=True))
        inv = jnp.where(t0 + r_col >= CMP_LEN - 1, 1.0 / jnp.sum(e, axis=-1, keepdims=True), 0.0)
        p = e * inv
        o_cmp = _dot(p.astype(BF16), vc_ref[0, 0])

        ps = p[0:tq]
        for h in range(1, hg):
            ps = ps + p[h * tq:(h + 1) * tq]
        p_hi, p_lo = _split_bf16(ps)
        w = wimp_ref[...]
        imp = _dot_nt(w, p_hi) + _dot_nt(w, p_lo)

        cur = (t0 + lane_t) >> 6
        forced = (blk == 0) | (blk == cur) | (blk == cur - 1)
        key = jnp.where(forced, -2.0, jnp.where(blk <= cur, imp, -1.0))
        sel = jnp.zeros((MAX_SEL_BLOCKS, tq), F32)
        for _ in range(SEL_TOPN - 3):
            mx = jnp.max(key, axis=0, keepdims=True)
            first = jnp.min(jnp.where(key == mx, blk, MAX_SEL_BLOCKS), axis=0, keepdims=True)
            hit = blk == first
            sel = jnp.where(hit, 1.0, sel)
            key = jnp.where(hit, -2.0, key)
        bias_t = jnp.where(((sel > 0.5) | forced) & (blk < cur), 0.0, NEG)
        bias_ref[0, 0, pl.ds(off, tq), :] = jnp.transpose(bias_t).astype(BF16)

        nchunk = wk // tq
        qi = first_tile + i
        full = t0 >= WINDOW
        band = []
        for a in range(nchunk):
            steady = tri_lo if a == 0 else (tri_hi if a == nchunk - 1 else zero)
            clipped = jnp.where(a < qi, zero, jnp.where(a == qi, tri_hi, NEG))
            band.append(jnp.where(full, steady, clipped))
        s = sw_ref[...] + heads(jnp.concatenate(band, axis=1))
        pw = jnp.exp2(s - jnp.max(s, axis=-1, keepdims=True))
        ow = _dot(pw.astype(BF16), vw_ref[0, 0, pl.ds(window_start(t0), wk), :])
        o_win = ow[:, 0:HEAD_DIM] / ow[:, HEAD_DIM:HEAD_DIM + 1]

        gt = gt_ref[0, 0, pl.ds(off, tq), :]
        parts = []
        for h in range(hg):
            g_cmp = gt[:, N_BRANCH * h:N_BRANCH * h + 1]
            g_win = gt[:, N_BRANCH * h + 2:N_BRANCH * h + 3]
            parts.append(g_cmp * o_cmp[h * tq:(h + 1) * tq, 0:HEAD_DIM] + g_win * o_win[h * tq:(h + 1) * tq])
        ocw_ref[0, pl.ds(off, tq), :] = jnp.concatenate(parts, axis=1)

    scores(0, sca_ref, swa_ref)

    def body(j, carry):
        i = 2 * j
        scores(i + 1, scb_ref, swb_ref)
        finish(i, sca_ref, swa_ref)
        scores(jnp.minimum(i + 2, nq - 1), sca_ref, swa_ref)
        finish(i + 1, scb_ref, swb_ref)
        return carry

    lax.fori_loop(0, nq // 2, body, 0)


def _branches(q_n, q_r, kc, vc, wimp_t, kw, vw_aug, gates):
    B, T, _ = q_n.shape
    G = N_KV_GROUPS
    nc = kc.shape[2]
    tq = Q_TILE
    tb = min(T, BRANCH_TILES * tq)
    assert T % tb == 0 and (tb // tq) % 2 == 0
    gw = HEADS_PER_GROUP * HEAD_DIM
    rows = HEADS_PER_GROUP * tq
    tokg = lambda: pl.BlockSpec((1, tb, gw), lambda b, g, i: (b, i, g))
    grp = lambda: pl.BlockSpec((1, 1, tb, LANES), lambda b, g, i: (b, g, i, 0))
    res = lambda n: pl.BlockSpec((1, 1, n, LANES), lambda b, g, i: (b, g, 0, 0))
    return pl.pallas_call(
        _branches_kernel,
        grid=(B, G, T // tb),
        in_specs=[tokg(), tokg(), res(nc), res(nc),
                  pl.BlockSpec((MAX_SEL_BLOCKS, nc), lambda b, g, i: (0, 0)),
                  res(T), res(T), grp()],
        out_specs=[tokg(), grp()],
        out_shape=[jax.ShapeDtypeStruct((B, T, ATTN_WIDTH), F32),
                   jax.ShapeDtypeStruct((B, G, T, LANES), BF16)],
        scratch_shapes=[pltpu.VMEM((rows, nc), F32), pltpu.VMEM((rows, nc), F32),
                        pltpu.VMEM((rows, WINDOW + tq), F32), pltpu.VMEM((rows, WINDOW + tq), F32)],
        compiler_params=_cparams(("parallel", "parallel", "arbitrary")),
        name="branches",
    )(q_n, q_r, kc, vc, wimp_t, kw, vw_aug, gates)


def _attn_kernel(qr_ref, bias_ref, ks_ref, vs_ref, ocw_ref, gt_ref, sza_ref, out_ref,
                 qa_ref, m_ref, acc_ref, sa_ref, sb_ref):
    tq = qr_ref.shape[1]
    hg = HEADS_PER_GROUP
    rows = hg * tq
    tk = K_TILE
    t0 = pl.multiple_of(pl.program_id(2) * tq, tq)
    q = qr_ref[0]
    bias = bias_ref[0, 0]
    zpad = jnp.zeros((tq, LANES - HEAD_DIM), BF16)
    for h in range(hg):
        qa_ref[h * tq:(h + 1) * tq, 0:LANES] = bias
        qa_ref[h * tq:(h + 1) * tq, LANES:2 * LANES] = jnp.concatenate(
            [q[:, h * HEAD_DIM:(h + 1) * HEAD_DIM], zpad], axis=1)

    def qk(kt, dst_ref):
        k0 = pl.multiple_of(kt * tk, tk)
        dst_ref[...] = _dot_nt(qa_ref[...], ks_ref[0, 0, pl.ds(k0, tk), :])

    def process(src_ref, kt):
        k0 = pl.multiple_of(kt * tk, tk)
        s = src_ref[...]
        m_prev = m_ref[...]
        m_new = jnp.maximum(m_prev, jnp.max(s, axis=-1, keepdims=True))
        alpha = jnp.exp2(m_prev - m_new)
        p = jnp.exp2(s - m_new[:, 0:1])
        acc_ref[...] = acc_ref[...] * alpha + _dot(p.astype(BF16), vs_ref[0, 0, pl.ds(k0, tk), :])
        m_ref[...] = m_new

    n = t0 // tk + 1
    last = ks_ref.shape[2] // tk - 1
    qk(0, sa_ref)

    r = lax.broadcasted_iota(jnp.int32, (tq, tq), 0)
    c = lax.broadcasted_iota(jnp.int32, (tq, tq), 1)
    own = jnp.where(((c >> 6) == (r >> 6)) & (c <= r), 0.0, NEG)
    s = _dot_nt(qa_ref[:, LANES:LANES + HEAD_DIM], ks_ref[0, 0, pl.ds(t0, tq), LANES:LANES + HEAD_DIM])
    s = s + jnp.concatenate([own] * hg, axis=0)
    m0 = jnp.max(s, axis=-1, keepdims=True)
    acc_ref[...] = _dot(jnp.exp2(s - m0).astype(BF16), vs_ref[0, 0, pl.ds(t0, tq), :])
    m_ref[...] = jnp.broadcast_to(m0, m_ref.shape)

    def body(j, carry):
        k0 = 2 * j
        qk(k0 + 1, sb_ref)
        process(sa_ref, k0)
        qk(jnp.minimum(k0 + 2, last), sa_ref)
        process(sb_ref, k0 + 1)
        return carry

    lax.fori_loop(0, (n + 1) // 2, body, 0)
    acc = acc_ref[...]
    o_slc = acc[:, 0:HEAD_DIM] / acc[:, HEAD_DIM:HEAD_DIM + 1]

    gt = gt_ref[0, 0]
    parts = [gt[:, N_BRANCH * h + 1:N_BRANCH * h + 2] * o_slc[h * tq:(h + 1) * tq] for h in range(hg)]
    attn = (ocw_ref[0] + jnp.concatenate(parts, axis=1)) * sza_ref[0].astype(F32)
    out_ref[0] = attn.astype(BF16)


def _attn(q_r, bias, ks_aug, vs_aug, o_cw, gates, sza):
    B, T, _ = q_r.shape
    G = N_KV_GROUPS
    tq = ATTN_Q_TILE
    gw = HEADS_PER_GROUP * HEAD_DIM
    rows = HEADS_PER_GROUP * tq
    tokg = lambda: pl.BlockSpec((1, tq, gw), lambda b, g, i: (b, i, g))
    grp = lambda w: pl.BlockSpec((1, 1, tq, w), lambda b, g, i: (b, g, i, 0))
    res = lambda w: pl.BlockSpec((1, 1, T, w), lambda b, g, i: (b, g, 0, 0))
    return pl.pallas_call(
        _attn_kernel,
        grid=(B, G, T // tq),
        in_specs=[tokg(), grp(LANES), res(2 * LANES), res(LANES), tokg(), grp(LANES), tokg()],
        out_specs=tokg(),
        out_shape=jax.ShapeDtypeStruct((B, T, ATTN_WIDTH), BF16),
        scratch_shapes=[pltpu.VMEM((rows, 2 * LANES), BF16),
                        pltpu.VMEM((rows, LANES), F32),
                        pltpu.VMEM((rows, LANES), F32),
                        pltpu.VMEM((rows, K_TILE), F32),
                        pltpu.VMEM((rows, K_TILE), F32)],
        compiler_params=_cparams(("parallel", "parallel", "arbitrary")),
        name="attn",
    )(q_r, bias, ks_aug, vs_aug, o_cw, gates, sza)


def _block_transpose8(xs):
    lane = lax.broadcasted_iota(jnp.int32, xs[0].shape, 1)
    xs = list(xs)
    for d in (4, 2, 1):
        w = d * SSM_GROUP
        keep = (lane & w) == 0
        for i in range(8):
            if i & d:
                continue
            lo, hi = xs[i], xs[i + d]
            xs[i] = jnp.where(keep, lo, pltpu.roll(hi, w, 1))
            xs[i + d] = jnp.where(keep, pltpu.roll(lo, LANES - w, 1), hi)
    return xs


def _s5_fold_kernel(u_ref, rh_ref, rl_ref, ug_ref, z_ref, uf_ref):
    nch = z_ref.shape[0]
    gw = S5_STEP * SSM_GROUP
    sw = 2 * SSM_STATE
    uf_ref[...] = u_ref[0].astype(F32)
    cols = [uf_ref[pl.ds(s, nch, stride=S5_STEP), :] for s in range(S5_STEP)]
    halves = [_block_transpose8(cols[0:8]), _block_transpose8(cols[8:16])]
    for g in range(LANES // SSM_GROUP):
        ug = jnp.concatenate([halves[0][g], halves[1][g]], axis=1).astype(BF16)
        ug_ref[:, g * gw:(g + 1) * gw] = ug
        z_ref[:, g * sw:(g + 1) * sw] = _dot(ug, rh_ref[g]) + _dot(ug, rl_ref[g])


SCAN_WAYS = 4


def _s5_scan_kernel(*refs, batch):
    z_refs, (a1_ref, a2_ref), s_refs = refs[:SCAN_WAYS], refs[SCAN_WAYS:SCAN_WAYS + 2], refs[SCAN_WAYS + 2:]
    nch = z_refs[0].shape[0] // batch
    a1 = [a1_ref[:, i * LANES:(i + 1) * LANES] for i in range(SCAN_WAYS)]
    a2 = [a2_ref[:, i * LANES:(i + 1) * LANES] for i in range(SCAN_WAYS)]

    def step(c, carry):
        out = []
        for i in range(SCAN_WAYS):
            s, t = carry[i]
            s_refs[i][pl.ds(c, batch, stride=nch), :] = s
            z = z_refs[i][pl.ds(c, batch, stride=nch), :]
            out.append((a1[i] * s + a2[i] * t + z, a1[i] * t - a2[i] * s + pltpu.roll(z, SSM_STATE, 1)))
        return tuple(out)

    zero = jnp.zeros((batch, LANES), F32)
    lax.fori_loop(0, nch, step, tuple((zero, zero) for _ in range(SCAN_WAYS)), unroll=8)


def _s5_out_kernel(ug_ref, *refs):
    s_refs, (mh_ref, ml_ref, oh_ref, ol_ref, y_ref) = refs[:SCAN_WAYS], refs[SCAN_WAYS:]
    nch = ug_ref.shape[0]
    gw = S5_STEP * SSM_GROUP
    sw = 2 * SSM_STATE
    ys = []
    for g in range(LANES // SSM_GROUP):
        ug = ug_ref[:, g * gw:(g + 1) * gw]
        way_col = g // SCAN_WAYS
        sh, sl = _split_bf16(s_refs[g % SCAN_WAYS][:, way_col * sw:(way_col + 1) * sw])
        y = _dot(ug, mh_ref[g]) + _dot(ug, ml_ref[g])
        ys.append(y + (_dot(sh, oh_ref[g]) + (_dot(sl, oh_ref[g]) + _dot(sh, ol_ref[g]))))
    for k in range(2):
        cols = _block_transpose8([y[:, k * LANES:(k + 1) * LANES] for y in ys])
        for t8 in range(8):
            y_ref[0, pl.ds(8 * k + t8, nch, stride=S5_STEP), :] = cols[t8]


def _s5(u, r_hi, r_lo, m_hi, m_lo, o_hi, o_lo, a1, a2):
    B, T, _ = u.shape
    ng = SSM_GROUPS
    nch = T // S5_STEP
    gw = S5_STEP * SSM_GROUP
    sw = 2 * SSM_STATE
    gpl = LANES // SSM_GROUP
    nslab = ng // gpl
    ops = lambda r, c: pl.BlockSpec((gpl, r, c), lambda b, j: (j, 0, 0))
    ug, z = pl.pallas_call(
        _s5_fold_kernel,
        grid=(B, nslab),
        in_specs=[pl.BlockSpec((1, T, LANES), lambda b, j: (b, 0, j)), ops(gw, sw), ops(gw, sw)],
        out_specs=[pl.BlockSpec((nch, gpl * gw), lambda b, j: (b, j)),
                   pl.BlockSpec((nch, gpl * sw), lambda b, j: (b, j))],
        out_shape=[jax.ShapeDtypeStruct((B * nch, ng * gw), BF16),
                   jax.ShapeDtypeStruct((B * nch, ng * sw), F32)],
        scratch_shapes=[pltpu.VMEM((T, LANES), F32)],
        compiler_params=_cparams(("parallel", "parallel")),
        name="s5_fold",
    )(u, r_hi, r_lo)
    ways = SCAN_WAYS
    col = lambda i: pl.BlockSpec((B * nch, sw), lambda j: (0, ways * j + i))
    coef = pl.BlockSpec((1, ways * sw), lambda j: (0, j))
    s_ways = pl.pallas_call(
        functools.partial(_s5_scan_kernel, batch=B),
        grid=(ng // ways,),
        in_specs=[col(i) for i in range(ways)] + [coef, coef],
        out_specs=[pl.BlockSpec((B * nch, sw), lambda j: (0, j)) for _ in range(ways)],
        out_shape=[jax.ShapeDtypeStruct((B * nch, ng // ways * sw), F32) for _ in range(ways)],
        compiler_params=_cparams(("parallel",)),
        name="s5_scan",
    )(*([z] * ways), a1, a2)
    per_way = gpl // ways
    return pl.pallas_call(
        _s5_out_kernel,
        grid=(B, nslab),
        in_specs=[pl.BlockSpec((nch, gpl * gw), lambda b, j: (b, j))]
                 + [pl.BlockSpec((nch, per_way * sw), lambda b, j: (b, j)) for _ in range(ways)]
                 + [ops(gw, gw), ops(gw, gw), ops(sw, gw), ops(sw, gw)],
        out_specs=pl.BlockSpec((1, T, LANES), lambda b, j: (b, 0, j)),
        out_shape=jax.ShapeDtypeStruct((B, T, ng * SSM_GROUP), F32),
        compiler_params=_cparams(("parallel", "parallel")),
        name="s5_out",
    )(ug, *s_ways, m_hi, m_lo, o_hi, o_lo)


def _cmul(ar, ai, br, bi):
    return ar * br - ai * bi, ar * bi + ai * br


def _s5_operators(a_re, a_im, log_dt, b_re, b_im, c_re, c_im):
    hp = lax.Precision.HIGHEST
    L = S5_STEP
    dt = jnp.exp(log_dt)[:, None]
    mag = jnp.exp(a_re * dt)
    lr, li = mag * jnp.cos(a_im * dt), mag * jnp.sin(a_im * dt)
    den = a_re * a_re + a_im * a_im
    fr, fi = _cmul(lr - 1.0, li, a_re / den, -a_im / den)
    bbr, bbi = _cmul(fr[..., None], fi[..., None], b_re, b_im)
    pr, pi = [jnp.ones_like(lr)], [jnp.zeros_like(li)]
    for _ in range(L):
        nr, ni = _cmul(pr[-1], pi[-1], lr, li)
        pr.append(nr)
        pi.append(ni)
    pr, pi = jnp.stack(pr), jnp.stack(pi)
    cpr, cpi = _cmul(c_re[None], c_im[None], pr[:L, :, None, :], pi[:L, :, None, :])
    kd = (jnp.einsum('dgcp,gpe->dgce', cpr, bbr, precision=hp)
          - jnp.einsum('dgcp,gpe->dgce', cpi, bbi, precision=hp))
    lag = jnp.arange(L)[None, :] - jnp.arange(L)[:, None]
    km = jnp.where((lag >= 0)[:, :, None, None, None], kd[jnp.clip(lag, 0, L - 1)], 0.0)
    ng, cg = b_re.shape[0], b_re.shape[2]
    m_op = km.transpose(2, 0, 4, 1, 3).reshape(ng, L * cg, L * cg)
    rr, ri = _cmul(pr[L - 1 - jnp.arange(L)][..., None], pi[L - 1 - jnp.arange(L)][..., None],
                   bbr[None], bbi[None])
    r_op = jnp.concatenate([rr, ri], axis=2).transpose(1, 0, 3, 2).reshape(ng, L * cg, 2 * SSM_STATE)
    orr, oii = _cmul(c_re[None], c_im[None], pr[1:, :, None, :], pi[1:, :, None, :])
    o_op = jnp.concatenate([orr, -oii], axis=3).transpose(1, 3, 0, 2).reshape(ng, 2 * SSM_STATE, L * cg)
    a1 = jnp.concatenate([pr[L], pr[L]], axis=1).reshape(1, -1)
    a2 = jnp.concatenate([-pi[L], pi[L]], axis=1).reshape(1, -1)
    return m_op, r_op, o_op, a1, a2


def _final_kernel(attn_ref, y_ref, u_ref, szs_ref, x_ref, mod_ref, d_ref, gw_ref, gb_ref, wo_ref, o_ref):
    yv = y_ref[0] + d_ref[...] * u_ref[0].astype(F32)
    yg = jax.nn.gelu(yv)
    gl = jax.nn.sigmoid(_dot(yg.astype(BF16), gw_ref[...]) + gb_ref[...])
    ssm = (yg * gl) * szs_ref[0].astype(F32)
    mix = _dot(attn_ref[0], wo_ref[0:ATTN_WIDTH, :]) + _dot(ssm.astype(BF16), wo_ref[ATTN_WIDTH:D_MODEL, :])
    o_ref[0] = x_ref[0] + mod_ref[0, 2:3, :] * mix


def _final(attn, y, u, szs, x, mod3, d_skip, glu_w, glu_b, w_out):
    B, T, D = x.shape
    tm = ROW_TILE
    tok = lambda w: pl.BlockSpec((1, tm, w), lambda b, i: (b, i, 0))
    const = lambda shape: pl.BlockSpec(shape, lambda b, i: tuple(0 for _ in shape))
    return pl.pallas_call(
        _final_kernel,
        grid=(B, T // tm),
        in_specs=[tok(ATTN_WIDTH), tok(SSM_WIDTH), tok(SSM_WIDTH), tok(SSM_WIDTH), tok(D),
                  pl.BlockSpec((1, 3, D), lambda b, i: (b, 0, 0)),
                  const((1, SSM_WIDTH)), const((SSM_WIDTH, SSM_WIDTH)), const((1, SSM_WIDTH)), const((D, D))],
        out_specs=tok(D),
        out_shape=jax.ShapeDtypeStruct((B, T, D), F32),
        compiler_params=_cparams(("parallel", "parallel")),
        name="final",
    )(attn, y, u, szs, x, mod3, d_skip, glu_w, glu_b, w_out)


def _rope_tables(T):
    inv = 1.0 / (ROPE_THETA ** (jnp.arange(0, HEAD_DIM, 2, dtype=F32) / HEAD_DIM))
    ang = jnp.arange(T, dtype=F32)[:, None] * inv[None, :]
    cos, sin = jnp.cos(ang), jnp.sin(ang)
    reps = LANES // HEAD_DIM
    return jnp.tile(jnp.concatenate([cos, cos], axis=1), (1, reps)), jnp.tile(jnp.concatenate([-sin, sin], axis=1), (1, reps))


def _importance_weights(nc):
    r = SEL_BLOCK // CMP_STRIDE
    ov = CMP_LEN // CMP_STRIDE
    w = np.zeros((MAX_SEL_BLOCKS, nc), np.float32)
    for j in range(MAX_SEL_BLOCKS):
        for m in range(r):
            for n in range(ov):
                i = r * j + m - n
                if 0 <= i < nc:
                    w[j, i] += 1.0
    return jnp.asarray(w, BF16)


def _layer(x, c, w_ada, b_ada, norm_g, w_in, q_norm_g, k_cmp_norm_g, k_slc_norm_g, k_win_norm_g,
           cmp_pos_k, cmp_pos_v, cmp_w1_k, cmp_w2_k, cmp_w1_v, cmp_w2_v,
           ssm_a_re, ssm_a_im, ssm_log_dt, ssm_b_re, ssm_b_im, ssm_c_re, ssm_c_im, ssm_d,
           glu_w, glu_b, w_out):
    B, T, D = x.shape
    G = N_KV_GROUPS
    assert D == D_MODEL and T % K_TILE == 0 and T >= WINDOW + Q_TILE and T % ATTN_Q_TILE == 0 and T // SEL_BLOCK <= MAX_SEL_BLOCKS
    assert T // SEL_BLOCK >= SEL_TOPN

    mod3 = _adaln(c, w_ada, b_ada).reshape(B, 3, D)

    o_gbr = ATTN_WIDTH + 6 * KV_WIDTH + ATTN_WIDTH
    o_u = o_gbr + N_BRANCH * N_HEADS
    npg = N_BRANCH * HEADS_PER_GROUP
    gate_cols = [jnp.pad(w_in[:, o_gbr + g * npg:o_gbr + (g + 1) * npg], ((0, 0), (0, LANES - npg))) for g in range(G)]
    w_pad = jnp.concatenate([w_in[:, :o_gbr], w_in[:, o_u:]] + gate_cols, axis=1).astype(BF16)
    assert w_pad.shape[1] == IN_PAD

    cos_t, sin_t = _rope_tables(T)
    hh = np.arange(LANES) // HEAD_DIM
    ones2 = jnp.asarray(hh[:, None] == hh[None, :], BF16)
    tile2 = lambda g: jnp.tile(g, LANES // HEAD_DIM).reshape(1, LANES)
    (q_n, q_r, kcr, vcr, ks_aug, vs_aug, kw, vw_aug, sza, u, szs, gates) = _inproj(
        x, mod3, norm_g.reshape(1, D), w_pad, cos_t, sin_t, ones2,
        tile2(q_norm_g), tile2(k_slc_norm_g), tile2(k_win_norm_g))

    ns = T // CMP_STRIDE
    half = CMP_STRIDE * HEAD_DIM

    def seg_weight(w1_half):
        w4 = w1_half.reshape(CMP_STRIDE, 1, HEAD_DIM, 1, CMP_HIDDEN)
        eye = jnp.eye(G, dtype=F32).reshape(1, G, 1, G, 1)
        return (w4 * eye).reshape(CMP_STRIDE * G * HEAD_DIM, G * CMP_HIDDEN).astype(BF16)

    def out_weight(w2):
        eye = jnp.eye(G, dtype=F32).reshape(G, 1, G, 1)
        return (w2.reshape(1, CMP_HIDDEN, 1, HEAD_DIM) * eye).reshape(G * CMP_HIDDEN, G * HEAD_DIM).astype(BF16)

    pos_rows = lambda p: jnp.broadcast_to(p.reshape(1, CMP_LEN * HEAD_DIM), (8, CMP_LEN * HEAD_DIM))
    kc, vc = _compress(
        kcr.reshape(B, ns, CMP_STRIDE * KV_WIDTH), vcr.reshape(B, ns, CMP_STRIDE * KV_WIDTH),
        seg_weight(cmp_w1_k[:half]), seg_weight(cmp_w1_k[half:]),
        seg_weight(cmp_w1_v[:half]), seg_weight(cmp_w1_v[half:]),
        cmp_w1_k, cmp_w1_v, pos_rows(cmp_pos_k), pos_rows(cmp_pos_v),
        out_weight(cmp_w2_k), out_weight(cmp_w2_v), tile2(k_cmp_norm_g), ones2)

    o_cw, bias = _branches(q_n, q_r, kc, vc, _importance_weights(ns), kw, vw_aug, gates)
    attn = _attn(q_r, bias, ks_aug, vs_aug, o_cw, gates, sza)

    m_op, r_op, o_op, a1, a2 = _s5_operators(ssm_a_re, ssm_a_im, ssm_log_dt, ssm_b_re, ssm_b_im, ssm_c_re, ssm_c_im)
    y = _s5(u, *_split_bf16(r_op), *_split_bf16(m_op), *_split_bf16(o_op), a1, a2)

    return _final(attn, y, u, szs, x, mod3, ssm_d.reshape(1, SSM_WIDTH), glu_w.astype(BF16),
                  glu_b.reshape(1, SSM_WIDTH), w_out.astype(BF16))


def kernel(x, c, w_ada, b_ada, norm_g, w_in, q_norm_g, k_cmp_norm_g, k_slc_norm_g, k_win_norm_g, cmp_pos_k, cmp_pos_v, cmp_w1_k, cmp_w2_k, cmp_w1_v, cmp_w2_v, ssm_a_re, ssm_a_im, ssm_log_dt, ssm_b_re, ssm_b_im, ssm_c_re, ssm_c_im, ssm_d, glu_w, glu_b, w_out):
    params = (w_ada, b_ada, norm_g, w_in, q_norm_g, k_cmp_norm_g, k_slc_norm_g, k_win_norm_g, cmp_pos_k, cmp_pos_v,
              cmp_w1_k, cmp_w2_k, cmp_w1_v, cmp_w2_v, ssm_a_re, ssm_a_im, ssm_log_dt, ssm_b_re, ssm_b_im,
              ssm_c_re, ssm_c_im, ssm_d, glu_w, glu_b, w_out)
    for l in range(w_ada.shape[0]):
        x = _layer(x, c, *(p[l] for p in params))
    return x
```

```python
import functools
import math

import jax
import jax.numpy as jnp
import numpy as np
from jax import lax
from jax.experimental import pallas as pl
from jax.experimental.pallas import tpu as pltpu

F32 = jnp.float32
BF16 = jnp.bfloat16

D_MODEL = 1024
ATTN_WIDTH = 512
N_HEADS = 8
HEAD_DIM = 64
N_KV_GROUPS = 2
HEADS_PER_GROUP = N_HEADS // N_KV_GROUPS
KV_WIDTH = N_KV_GROUPS * HEAD_DIM
CMP_LEN = 32
CMP_STRIDE = 16
CMP_HIDDEN = 256
SEL_BLOCK = 64
SEL_TOPN = 16
WINDOW = 512
N_BRANCH = 3
ROPE_THETA = 10000.0
SSM_WIDTH = D_MODEL - ATTN_WIDTH
SSM_GROUP = 16
SSM_GROUPS = SSM_WIDTH // SSM_GROUP
SSM_STATE = 64
EPS = 1e-6

LANES = 128
MAX_SEL_BLOCKS = LANES
NEG = -1e30
Q_SCALE = HEAD_DIM ** -0.5 * math.log2(math.e)
VMEM_LIMIT = 56 * 1024 * 1024

S5_STEP = 16
ROW_TILE = 512
Q_TILE = 128
ATTN_Q_TILE = 256
K_TILE = 512
BRANCH_TILES = 16

_C_Q, _C_KC, _C_VC, _C_KS, _C_VS, _C_KW, _C_VW = 0, 512, 640, 768, 896, 1024, 1152
_C_ZA, _C_U, _C_ZS, _C_GT = 1280, 1792, 2304, 2816
IN_PAD = _C_GT + N_KV_GROUPS * LANES


def _cparams(sem):
    return pltpu.CompilerParams(dimension_semantics=sem, vmem_limit_bytes=VMEM_LIMIT)


def _split_bf16(a):
    hi = a.astype(BF16)
    lo = (a - hi.astype(F32)).astype(BF16)
    return hi, lo


def _dot(a, b):
    return jnp.dot(a, b, preferred_element_type=F32)


def _dot_nt(a, b):
    return lax.dot_general(a, b, (((1,), (1,)), ((), ())), preferred_element_type=F32)


def _dot_f32(a, b):
    ah, al = _split_bf16(a)
    bh, bl = _split_bf16(b)
    return _dot(ah, bh) + (_dot(al, bh) + _dot(ah, bl))


def _adaln_kernel(c_ref, w_ref, b_ref, o_ref):
    c = c_ref[...]
    o_ref[...] = _dot_f32(jax.nn.silu(c), w_ref[...]) + b_ref[...]


def _adaln(c, w_ada, b_ada):
    B, D = c.shape
    n = w_ada.shape[1]
    tn = 1024
    return pl.pallas_call(
        _adaln_kernel,
        grid=(n // tn,),
        in_specs=[pl.BlockSpec((B, D), lambda j: (0, 0)),
                  pl.BlockSpec((D, tn), lambda j: (0, j)),
                  pl.BlockSpec((1, tn), lambda j: (0, j))],
        out_specs=pl.BlockSpec((B, tn), lambda j: (0, j)),
        out_shape=jax.ShapeDtypeStruct((B, n), F32),
        compiler_params=_cparams(("arbitrary",)),
        name="adaln",
    )(c, w_ada, b_ada.reshape(1, n))


def _head_norm(v, ones_ref, gvec):
    ss = _dot((v * v).astype(BF16), ones_ref[...])
    return v * lax.rsqrt(ss * (1.0 / HEAD_DIM) + EPS) * gvec


def _inproj_kernel(x_ref, mod_ref, ng_ref, w_ref, cos_ref, sin_ref, ones_ref, gq_ref, gks_ref, gkw_ref,
                   qn_ref, qr_ref, kc_ref, vc_ref, ksa_ref, vsa_ref, kw_ref, vwa_ref,
                   sza_ref, u_ref, szs_ref, gt_ref):
    tm = x_ref.shape[1]
    ti = pl.program_id(1)
    x = x_ref[0]
    ms = jnp.mean(x * x, axis=-1, keepdims=True)
    shift = mod_ref[0, 0:1, :]
    scale = mod_ref[0, 1:2, :]
    h = (x * lax.rsqrt(ms + EPS)) * ng_ref[...] * (1.0 + scale) + shift
    hb = h.astype(BF16)

    def proj(c0, width):
        return _dot(hb, w_ref[:, c0:c0 + width])

    cosv = cos_ref[...]
    sinv = sin_ref[...]
    lane = lax.broadcasted_iota(jnp.int32, (tm, LANES), 1)
    row = lax.broadcasted_iota(jnp.int32, (tm, LANES), 0)
    first_half = (lane & (HEAD_DIM - 1)) < (HEAD_DIM // 2)
    low = lane < HEAD_DIM

    def rope(v):
        sw = jnp.where(first_half, pltpu.roll(v, LANES - HEAD_DIM // 2, 1), pltpu.roll(v, HEAD_DIM // 2, 1))
        return v * cosv + sw * sinv

    def group_part(v, g):
        return v if g == 0 else pltpu.roll(v, HEAD_DIM, 1)

    def proj_pair(c0):
        pair = proj(c0, 2 * LANES)
        return pair[:, 0:LANES], pair[:, LANES:2 * LANES]

    for c2 in range(ATTN_WIDTH // (2 * LANES)):
        for c, q in zip((2 * c2, 2 * c2 + 1), proj_pair(_C_Q + 2 * c2 * LANES)):
            qn = _head_norm(q, ones_ref, gq_ref[...]) * Q_SCALE
            qn_ref[0, :, c * LANES:(c + 1) * LANES] = qn.astype(BF16)
            qr_ref[0, :, c * LANES:(c + 1) * LANES] = rope(qn).astype(BF16)

    kcr, vcr = proj_pair(_C_KC)
    kc_ref[0] = kcr.astype(BF16)
    vc_ref[0] = vcr.astype(BF16)

    blk = (ti * tm + row) >> 6
    onehot = jnp.where(lane == blk, 1.0, 0.0).astype(BF16)
    ones_col = jnp.where(lane == HEAD_DIM, 1.0, 0.0)

    ksl, vsl = proj_pair(_C_KS)
    kwn, vwn = proj_pair(_C_KW)
    ksl = rope(_head_norm(ksl, ones_ref, gks_ref[...]))
    kwn = rope(_head_norm(kwn, ones_ref, gkw_ref[...]))
    gate_logits = proj_pair(_C_GT)
    for g in range(N_KV_GROUPS):
        ksa_ref[0, g, :, 0:LANES] = onehot
        ksa_ref[0, g, :, LANES:2 * LANES] = jnp.where(low, group_part(ksl, g), 0.0).astype(BF16)
        vsa_ref[0, g] = jnp.where(low, group_part(vsl, g), ones_col).astype(BF16)
        kw_ref[0, g] = jnp.where(low, group_part(kwn, g), 0.0).astype(BF16)
        vwa_ref[0, g] = jnp.where(low, group_part(vwn, g), ones_col).astype(BF16)
        gt_ref[0, g] = jax.nn.sigmoid(gate_logits[g])

    sza_ref[0] = jax.nn.silu(proj(_C_ZA, ATTN_WIDTH)).astype(BF16)
    u_ref[0] = proj(_C_U, SSM_WIDTH).astype(BF16)
    szs_ref[0] = jax.nn.silu(proj(_C_ZS, SSM_WIDTH)).astype(BF16)


def _inproj(x, mod3, norm_g, w_pad, cos_t, sin_t, ones2, gq, gks, gkw):
    B, T, D = x.shape
    tm = ROW_TILE
    G = N_KV_GROUPS
    tok = lambda w: pl.BlockSpec((1, tm, w), lambda b, i: (b, i, 0))
    grp = lambda w: pl.BlockSpec((1, G, tm, w), lambda b, i: (b, 0, i, 0))
    const = lambda shape: pl.BlockSpec(shape, lambda b, i: tuple(0 for _ in shape))
    tshape = lambda w, dt: jax.ShapeDtypeStruct((B, T, w), dt)
    gshape = lambda w, dt: jax.ShapeDtypeStruct((B, G, T, w), dt)
    return pl.pallas_call(
        _inproj_kernel,
        grid=(B, T // tm),
        in_specs=[tok(D),
                  pl.BlockSpec((1, 3, D), lambda b, i: (b, 0, 0)),
                  const((1, D)),
                  const((D, IN_PAD)),
                  pl.BlockSpec((tm, LANES), lambda b, i: (i, 0)),
                  pl.BlockSpec((tm, LANES), lambda b, i: (i, 0)),
                  const((LANES, LANES)), const((1, LANES)), const((1, LANES)), const((1, LANES))],
        out_specs=[tok(ATTN_WIDTH), tok(ATTN_WIDTH), tok(LANES), tok(LANES),
                   grp(2 * LANES), grp(LANES), grp(LANES), grp(LANES),
                   tok(ATTN_WIDTH), tok(SSM_WIDTH), tok(SSM_WIDTH), grp(LANES)],
        out_shape=[tshape(ATTN_WIDTH, BF16), tshape(ATTN_WIDTH, BF16), tshape(LANES, BF16), tshape(LANES, BF16),
                   gshape(2 * LANES, BF16), gshape(LANES, BF16), gshape(LANES, BF16), gshape(LANES, BF16),
                   tshape(ATTN_WIDTH, BF16), tshape(SSM_WIDTH, BF16), tshape(SSM_WIDTH, BF16), gshape(LANES, F32)],
        compiler_params=_cparams(("parallel", "parallel")),
        name="inproj",
    )(x, mod3, norm_g, w_pad, cos_t, sin_t, ones2, gq, gks, gkw)


def _compress_kernel(kx_ref, vx_ref, wtk_ref, wbk_ref, wtv_ref, wbv_ref, w1k_ref, w1v_ref, posk_ref, posv_ref,
                     w2k_ref, w2v_ref, gk_ref, ones_ref, kc_ref, vc_ref):
    ns = kx_ref.shape[1]
    lane = lax.broadcasted_iota(jnp.int32, (ns, LANES), 1)
    row = lax.broadcasted_iota(jnp.int32, (ns, LANES), 0)
    low = lane < HEAD_DIM
    live = row < ns - 1

    def mlp(x_ref, wt_ref, wb_ref, w1_ref, pos_ref, w2_ref):
        x = x_ref[0]
        top = _dot(x, wt_ref[...])
        bot = pltpu.roll(_dot(x, wb_ref[...]), ns - 1, 0)
        b1 = _dot_f32(pos_ref[...], w1_ref[...])[0:1]
        hid = top + bot + jnp.concatenate([b1, b1], axis=1)
        return _dot(jax.nn.gelu(hid).astype(BF16), w2_ref[...])

    kc = mlp(kx_ref, wtk_ref, wbk_ref, w1k_ref, posk_ref, w2k_ref)
    kc = _head_norm(kc, ones_ref, gk_ref[...])
    vc = mlp(vx_ref, wtv_ref, wbv_ref, w1v_ref, posv_ref, w2v_ref)
    for g in range(N_KV_GROUPS):
        kg = kc if g == 0 else pltpu.roll(kc, HEAD_DIM, 1)
        vg = vc if g == 0 else pltpu.roll(vc, HEAD_DIM, 1)
        kc_ref[0, g] = jnp.where(low & live, kg, 0.0).astype(BF16)
        vc_ref[0, g] = jnp.where(low & live, vg, 0.0).astype(BF16)


def _compress(kx, vx, wtk, wbk, wtv, wbv, w1k, w1v, posk, posv, w2k, w2v, gk, ones2):
    B, ns, w = kx.shape
    G = N_KV_GROUPS
    const = lambda a: pl.BlockSpec(a.shape, lambda b: tuple(0 for _ in a.shape))
    consts = (wtk, wbk, wtv, wbv, w1k, w1v, posk, posv, w2k, w2v, gk, ones2)
    return pl.pallas_call(
        _compress_kernel,
        grid=(B,),
        in_specs=[pl.BlockSpec((1, ns, w), lambda b: (b, 0, 0)),
                  pl.BlockSpec((1, ns, w), lambda b: (b, 0, 0))] + [const(a) for a in consts],
        out_specs=[pl.BlockSpec((1, G, ns, LANES), lambda b: (b, 0, 0, 0)),
                   pl.BlockSpec((1, G, ns, LANES), lambda b: (b, 0, 0, 0))],
        out_shape=[jax.ShapeDtypeStruct((B, G, ns, LANES), BF16),
                   jax.ShapeDtypeStruct((B, G, ns, LANES), BF16)],
        compiler_params=_cparams(("parallel",)),
        name="compress",
    )(kx, vx, *consts)


def _stack_heads(q):
    return jnp.concatenate([q[:, h * HEAD_DIM:(h + 1) * HEAD_DIM] for h in range(HEADS_PER_GROUP)], axis=0)


def _branches_kernel(qn_ref, qr_ref, kc_ref, vc_ref, wimp_ref, kw_ref, vw_ref, gt_ref, ocw_ref, bias_ref,
                     sca_ref, scb_ref, swa_ref, swb_ref, *, nsteps):
    tq = Q_TILE
    hg = HEADS_PER_GROUP
    nq = qn_ref.shape[1] // tq
    wk = WINDOW + tq
    first_tile = pl.program_id(2) * nq
    heads = lambda slab: jnp.concatenate([slab] * hg, axis=0)

    def window_start(t0):
        return pl.multiple_of(jnp.maximum(t0 - WINDOW, 0), tq)

    def scores(i, sc_ref, sw_ref, nc):
        off = pl.multiple_of(i * tq, tq)
        t0 = (first_tile + i) * tq
        sc_ref[:, 0:nc] = _dot_nt(_stack_heads(qn_ref[0, pl.ds(off, tq), :]), kc_ref[0, 0, 0:nc, 0:HEAD_DIM])
        sw_ref[...] = _dot_nt(_stack_heads(qr_ref[0, pl.ds(off, tq), :]),
                              kw_ref[0, 0, pl.ds(window_start(t0), wk), 0:HEAD_DIM])

    def finish(i, sc_ref, sw_ref, nc):
        off = pl.multiple_of(i * tq, tq)
        t0 = (first_tile + i) * tq
        r = lax.broadcasted_iota(jnp.int32, (tq, tq), 0)
        c = lax.broadcasted_iota(jnp.int32, (tq, tq), 1)
        zero = jnp.zeros((tq, tq), F32)
        tri_lo = jnp.where(c > r, zero, NEG)
        tri_hi = jnp.where(c <= r, zero, NEG)
        cmp_gap = (lax.broadcasted_iota(jnp.int32, (tq, nc), 1) * CMP_STRIDE + (CMP_LEN - 1)
                   - lax.broadcasted_iota(jnp.int32, (tq, nc), 0))
        r_col = lax.broadcasted_iota(jnp.int32, (hg * tq, 1), 0) & (tq - 1)
        blk = lax.broadcasted_iota(jnp.int32, (MAX_SEL_BLOCKS, tq), 0)
        lane_t = lax.broadcasted_iota(jnp.int32, (MAX_SEL_BLOCKS, tq), 1)

        s = sc_ref[:, 0:nc] + heads(jnp.where(cmp_gap <= t0, 0.0, NEG))
        e = jnp.exp2(s - jnp.max(s, axis=-1, keepdims=True))
        inv = jnp.where(t0 + r_col >= CMP_LEN - 1, 1.0 / jnp.sum(e, axis=-1, keepdims=True), 0.0)
        p = e * inv
        o_cmp = _dot(p.astype(BF16), vc_ref[0, 0, 0:nc, :])

        ps = p[0:tq]
        for h in range(1, hg):
            ps = ps + p[h * tq:(h + 1) * tq]
        p_hi, p_lo = _split_bf16(ps)
        w = wimp_ref[:, 0:nc]
        imp = _dot_nt(w, p_hi) + _dot_nt(w, p_lo)

        cur = (t0 + lane_t) >> 6
        forced = (blk == 0) | (blk == cur) | (blk == cur - 1)
        key = jnp.where(forced, -2.0, jnp.where(blk <= cur, imp, -1.0))
        sel = jnp.zeros((MAX_SEL_BLOCKS, tq), F32)
        for _ in range(SEL_TOPN - 3):
            mx = jnp.max(key, axis=0, keepdims=True)
            first = jnp.min(jnp.where(key == mx, blk, MAX_SEL_BLOCKS), axis=0, keepdims=True)
            hit = blk == first
            sel = jnp.where(hit, 1.0, sel)
            key = jnp.where(hit, -2.0, key)
        bias_t = jnp.where(((sel > 0.5) | forced) & (blk <= cur), 0.0, NEG)
        bias_ref[0, 0, pl.ds(off, tq), :] = jnp.transpose(bias_t).astype(BF16)

        nchunk = wk // tq
        qi = first_tile + i
        full = t0 >= WINDOW
        band = []
        for a in range(nchunk):
            steady = tri_lo if a == 0 else (tri_hi if a == nchunk - 1 else zero)
            clipped = jnp.where(a < qi, zero, jnp.where(a == qi, tri_hi, NEG))
            band.append(jnp.where(full, steady, clipped))
        s = sw_ref[...] + heads(jnp.concatenate(band, axis=1))
        pw = jnp.exp2(s - jnp.max(s, axis=-1, keepdims=True))
        ow = _dot(pw.astype(BF16), vw_ref[0, 0, pl.ds(window_start(t0), wk), :])
        o_win = ow[:, 0:HEAD_DIM] / ow[:, HEAD_DIM:HEAD_DIM + 1]

        gt = gt_ref[0, 0, pl.ds(off, tq), :]
        parts = []
        for h in range(hg):
            g_cmp = gt[:, N_BRANCH * h:N_BRANCH * h + 1]
            g_win = gt[:, N_BRANCH * h + 2:N_BRANCH * h + 3]
            parts.append(g_cmp * o_cmp[h * tq:(h + 1) * tq, 0:HEAD_DIM] + g_win * o_win[h * tq:(h + 1) * tq])
        ocw_ref[0, pl.ds(off, tq), :] = jnp.concatenate(parts, axis=1)

    def run(nc):
        scores(0, sca_ref, swa_ref, nc)

        def body(j, carry):
            i = 2 * j
            scores(i + 1, scb_ref, swb_ref, nc)
            finish(i, sca_ref, swa_ref, nc)
            scores(jnp.minimum(i + 2, nq - 1), sca_ref, swa_ref, nc)
            finish(i + 1, scb_ref, swb_ref, nc)
            return carry

        lax.fori_loop(0, nq // 2, body, 0)

    def visible_cols(step):
        n_vis = ((step + 1) * nq * tq - CMP_LEN) // CMP_STRIDE + 1
        return min(kc_ref.shape[2], -(-n_vis // LANES) * LANES)

    lax.switch(pl.program_id(2), [functools.partial(run, visible_cols(step)) for step in range(nsteps)])


def _branches(q_n, q_r, kc, vc, wimp_t, kw, vw_aug, gates):
    B, T, _ = q_n.shape
    G = N_KV_GROUPS
    nc = kc.shape[2]
    tq = Q_TILE
    tb = min(T, BRANCH_TILES * tq)
    assert T % tb == 0 and (tb // tq) % 2 == 0
    gw = HEADS_PER_GROUP * HEAD_DIM
    rows = HEADS_PER_GROUP * tq
    tokg = lambda: pl.BlockSpec((1, tb, gw), lambda b, g, i: (b, i, g))
    grp = lambda: pl.BlockSpec((1, 1, tb, LANES), lambda b, g, i: (b, g, i, 0))
    res = lambda n: pl.BlockSpec((1, 1, n, LANES), lambda b, g, i: (b, g, 0, 0))
    return pl.pallas_call(
        functools.partial(_branches_kernel, nsteps=T // tb),
        grid=(B, G, T // tb),
        in_specs=[tokg(), tokg(), res(nc), res(nc),
                  pl.BlockSpec((MAX_SEL_BLOCKS, nc), lambda b, g, i: (0, 0)),
                  res(T), res(T), grp()],
        out_specs=[tokg(), grp()],
        out_shape=[jax.ShapeDtypeStruct((B, T, ATTN_WIDTH), F32),
                   jax.ShapeDtypeStruct((B, G, T, LANES), BF16)],
        scratch_shapes=[pltpu.VMEM((rows, nc), F32), pltpu.VMEM((rows, nc), F32),
                        pltpu.VMEM((rows, WINDOW + tq), F32), pltpu.VMEM((rows, WINDOW + tq), F32)],
        compiler_params=_cparams(("parallel", "parallel", "arbitrary")),
        name="branches",
    )(q_n, q_r, kc, vc, wimp_t, kw, vw_aug, gates)


def _attn_kernel(qr_ref, bias_ref, ks_ref, vs_ref, ocw_ref, gt_ref, sza_ref, out_ref,
                 qa_ref, m_ref, acc_ref, sa_ref, sb_ref):
    tq = qr_ref.shape[1]
    hg = HEADS_PER_GROUP
    rows = hg * tq
    tk = K_TILE
    t0 = pl.multiple_of(pl.program_id(2) * tq, tq)
    q = qr_ref[0]
    bias = bias_ref[0, 0]
    zpad = jnp.zeros((tq, LANES - HEAD_DIM), BF16)
    for h in range(hg):
        qa_ref[h * tq:(h + 1) * tq, 0:LANES] = bias
        qa_ref[h * tq:(h + 1) * tq, LANES:2 * LANES] = jnp.concatenate(
            [q[:, h * HEAD_DIM:(h + 1) * HEAD_DIM], zpad], axis=1)

    def qk(kt, dst_ref):
        k0 = pl.multiple_of(kt * tk, tk)
        dst_ref[...] = _dot_nt(qa_ref[...], ks_ref[0, 0, pl.ds(k0, tk), :])

    gap = (lax.broadcasted_iota(jnp.int32, (tq, tk), 1) - lax.broadcasted_iota(jnp.int32, (tq, tk), 0))

    def process(src_ref, kt, causal):
        k0 = pl.multiple_of(kt * tk, tk)
        s = src_ref[...]
        if causal:
            s = s + jnp.concatenate([jnp.where(gap > t0 - k0, NEG, 0.0)] * hg, axis=0)
        m_prev = m_ref[...]
        m_new = jnp.maximum(m_prev, jnp.max(s, axis=-1, keepdims=True))
        alpha = jnp.exp2(m_prev - m_new)
        p = jnp.exp2(s - m_new[:, 0:1])
        acc_ref[...] = acc_ref[...] * alpha + _dot(p.astype(BF16), vs_ref[0, 0, pl.ds(k0, tk), :])
        m_ref[...] = m_new

    n = t0 // tk + 1
    m_ref[...] = jnp.full(m_ref.shape, NEG, F32)
    acc_ref[...] = jnp.zeros(acc_ref.shape, F32)
    qk(0, sa_ref)

    def body(j, carry):
        k0 = 2 * j
        qk(k0 + 1, sb_ref)
        process(sa_ref, k0, False)
        qk(k0 + 2, sa_ref)
        process(sb_ref, k0 + 1, False)
        return carry

    full_pairs = (n + 1) // 2 - 1
    lax.fori_loop(0, full_pairs, body, 0)
    k0 = 2 * full_pairs
    qk(k0 + 1, sb_ref)
    process(sa_ref, k0, True)
    process(sb_ref, k0 + 1, True)
    acc = acc_ref[...]
    o_slc = acc[:, 0:HEAD_DIM] / acc[:, HEAD_DIM:HEAD_DIM + 1]

    gt = gt_ref[0, 0]
    parts = [gt[:, N_BRANCH * h + 1:N_BRANCH * h + 2] * o_slc[h * tq:(h + 1) * tq] for h in range(hg)]
    attn = (ocw_ref[0] + jnp.concatenate(parts, axis=1)) * sza_ref[0].astype(F32)
    out_ref[0] = attn.astype(BF16)


def _attn(q_r, bias, ks_aug, vs_aug, o_cw, gates, sza):
    B, T, _ = q_r.shape
    G = N_KV_GROUPS
    tq = ATTN_Q_TILE
    gw = HEADS_PER_GROUP * HEAD_DIM
    rows = HEADS_PER_GROUP * tq
    tokg = lambda: pl.BlockSpec((1, tq, gw), lambda b, g, i: (b, i, g))
    grp = lambda w: pl.BlockSpec((1, 1, tq, w), lambda b, g, i: (b, g, i, 0))
    res = lambda w: pl.BlockSpec((1, 1, T, w), lambda b, g, i: (b, g, 0, 0))
    return pl.pallas_call(
        _attn_kernel,
        grid=(B, G, T // tq),
        in_specs=[tokg(), grp(LANES), res(2 * LANES), res(LANES), tokg(), grp(LANES), tokg()],
        out_specs=tokg(),
        out_shape=jax.ShapeDtypeStruct((B, T, ATTN_WIDTH), BF16),
        scratch_shapes=[pltpu.VMEM((rows, 2 * LANES), BF16),
                        pltpu.VMEM((rows, LANES), F32),
                        pltpu.VMEM((rows, LANES), F32),
                        pltpu.VMEM((rows, K_TILE), F32),
                        pltpu.VMEM((rows, K_TILE), F32)],
        compiler_params=_cparams(("parallel", "parallel", "arbitrary")),
        name="attn",
    )(q_r, bias, ks_aug, vs_aug, o_cw, gates, sza)


def _block_transpose8(xs):
    lane = lax.broadcasted_iota(jnp.int32, xs[0].shape, 1)
    xs = list(xs)
    for d in (4, 2, 1):
        w = d * SSM_GROUP
        keep = (lane & w) == 0
        for i in range(8):
            if i & d:
                continue
            lo, hi = xs[i], xs[i + d]
            xs[i] = jnp.where(keep, lo, pltpu.roll(hi, w, 1))
            xs[i + d] = jnp.where(keep, pltpu.roll(lo, LANES - w, 1), hi)
    return xs


def _s5_fold_kernel(u_ref, rh_ref, rl_ref, ug_ref, z_ref, uf_ref):
    nch = z_ref.shape[0]
    gw = S5_STEP * SSM_GROUP
    sw = 2 * SSM_STATE
    uf_ref[...] = u_ref[0].astype(F32)
    cols = [uf_ref[pl.ds(s, nch, stride=S5_STEP), :] for s in range(S5_STEP)]
    halves = [_block_transpose8(cols[0:8]), _block_transpose8(cols[8:16])]
    for g in range(LANES // SSM_GROUP):
        ug = jnp.concatenate([halves[0][g], halves[1][g]], axis=1).astype(BF16)
        ug_ref[:, g * gw:(g + 1) * gw] = ug
        z_ref[:, g * sw:(g + 1) * sw] = _dot(ug, rh_ref[g]) + _dot(ug, rl_ref[g])


SCAN_WAYS = 4


def _s5_scan_kernel(*refs, batch):
    z_refs, (a1_ref, a2_ref), s_refs = refs[:SCAN_WAYS], refs[SCAN_WAYS:SCAN_WAYS + 2], refs[SCAN_WAYS + 2:]
    nch = z_refs[0].shape[0] // batch
    a1 = [a1_ref[:, i * LANES:(i + 1) * LANES] for i in range(SCAN_WAYS)]
    a2 = [a2_ref[:, i * LANES:(i + 1) * LANES] for i in range(SCAN_WAYS)]

    def step(c, carry):
        out = []
        for i in range(SCAN_WAYS):
            s, t = carry[i]
            s_refs[i][pl.ds(c, batch, stride=nch), :] = s
            z = z_refs[i][pl.ds(c, batch, stride=nch), :]
            out.append((a1[i] * s + a2[i] * t + z, a1[i] * t - a2[i] * s + pltpu.roll(z, SSM_STATE, 1)))
        return tuple(out)

    zero = jnp.zeros((batch, LANES), F32)
    lax.fori_loop(0, nch, step, tuple((zero, zero) for _ in range(SCAN_WAYS)), unroll=8)


def _s5_out_kernel(ug_ref, *refs):
    s_refs, (mh_ref, ml_ref, oh_ref, ol_ref, y_ref) = refs[:SCAN_WAYS], refs[SCAN_WAYS:]
    nch = ug_ref.shape[0]
    gw = S5_STEP * SSM_GROUP
    sw = 2 * SSM_STATE
    ys = []
    for g in range(LANES // SSM_GROUP):
        ug = ug_ref[:, g * gw:(g + 1) * gw]
        way_col = g // SCAN_WAYS
        sh, sl = _split_bf16(s_refs[g % SCAN_WAYS][:, way_col * sw:(way_col + 1) * sw])
        y = _dot(ug, mh_ref[g]) + _dot(ug, ml_ref[g])
        ys.append(y + (_dot(sh, oh_ref[g]) + (_dot(sl, oh_ref[g]) + _dot(sh, ol_ref[g]))))
    for k in range(2):
        cols = _block_transpose8([y[:, k * LANES:(k + 1) * LANES] for y in ys])
        for t8 in range(8):
            y_ref[0, pl.ds(8 * k + t8, nch, stride=S5_STEP), :] = cols[t8]


def _s5(u, r_hi, r_lo, m_hi, m_lo, o_hi, o_lo, a1, a2):
    B, T, _ = u.shape
    ng = SSM_GROUPS
    nch = T // S5_STEP
    gw = S5_STEP * SSM_GROUP
    sw = 2 * SSM_STATE
    gpl = LANES // SSM_GROUP
    nslab = ng // gpl
    ops = lambda r, c: pl.BlockSpec((gpl, r, c), lambda b, j: (j, 0, 0))
    ug, z = pl.pallas_call(
        _s5_fold_kernel,
        grid=(B, nslab),
        in_specs=[pl.BlockSpec((1, T, LANES), lambda b, j: (b, 0, j)), ops(gw, sw), ops(gw, sw)],
        out_specs=[pl.BlockSpec((nch, gpl * gw), lambda b, j: (b, j)),
                   pl.BlockSpec((nch, gpl * sw), lambda b, j: (b, j))],
        out_shape=[jax.ShapeDtypeStruct((B * nch, ng * gw), BF16),
                   jax.ShapeDtypeStruct((B * nch, ng * sw), F32)],
        scratch_shapes=[pltpu.VMEM((T, LANES), F32)],
        compiler_params=_cparams(("parallel", "parallel")),
        name="s5_fold",
    )(u, r_hi, r_lo)
    ways = SCAN_WAYS
    col = lambda i: pl.BlockSpec((B * nch, sw), lambda j: (0, ways * j + i))
    coef = pl.BlockSpec((1, ways * sw), lambda j: (0, j))
    s_ways = pl.pallas_call(
        functools.partial(_s5_scan_kernel, batch=B),
        grid=(ng // ways,),
        in_specs=[col(i) for i in range(ways)] + [coef, coef],
        out_specs=[pl.BlockSpec((B * nch, sw), lambda j: (0, j)) for _ in range(ways)],
        out_shape=[jax.ShapeDtypeStruct((B * nch, ng // ways * sw), F32) for _ in range(ways)],
        compiler_params=_cparams(("parallel",)),
        name="s5_scan",
    )(*([z] * ways), a1, a2)
    per_way = gpl // ways
    return pl.pallas_call(
        _s5_out_kernel,
        grid=(B, nslab),
        in_specs=[pl.BlockSpec((nch, gpl * gw), lambda b, j: (b, j))]
                 + [pl.BlockSpec((nch, per_way * sw), lambda b, j: (b, j)) for _ in range(ways)]
                 + [ops(gw, gw), ops(gw, gw), ops(sw, gw), ops(sw, gw)],
        out_specs=pl.BlockSpec((1, T, LANES), lambda b, j: (b, 0, j)),
        out_shape=jax.ShapeDtypeStruct((B, T, ng * SSM_GROUP), F32),
        compiler_params=_cparams(("parallel", "parallel")),
        name="s5_out",
    )(ug, *s_ways, m_hi, m_lo, o_hi, o_lo)


def _cmul(ar, ai, br, bi):
    return ar * br - ai * bi, ar * bi + ai * br


def _s5_operators(a_re, a_im, log_dt, b_re, b_im, c_re, c_im):
    hp = lax.Precision.HIGHEST
    L = S5_STEP
    dt = jnp.exp(log_dt)[:, None]
    mag = jnp.exp(a_re * dt)
    lr, li = mag * jnp.cos(a_im * dt), mag * jnp.sin(a_im * dt)
    den = a_re * a_re + a_im * a_im
    fr, fi = _cmul(lr - 1.0, li, a_re / den, -a_im / den)
    bbr, bbi = _cmul(fr[..., None], fi[..., None], b_re, b_im)
    pr, pi = [jnp.ones_like(lr)], [jnp.zeros_like(li)]
    for _ in range(L):
        nr, ni = _cmul(pr[-1], pi[-1], lr, li)
        pr.append(nr)
        pi.append(ni)
    pr, pi = jnp.stack(pr), jnp.stack(pi)
    cpr, cpi = _cmul(c_re[None], c_im[None], pr[:L, :, None, :], pi[:L, :, None, :])
    kd = (jnp.einsum('dgcp,gpe->dgce', cpr, bbr, precision=hp)
          - jnp.einsum('dgcp,gpe->dgce', cpi, bbi, precision=hp))
    lag = jnp.arange(L)[None, :] - jnp.arange(L)[:, None]
    km = jnp.where((lag >= 0)[:, :, None, None, None], kd[jnp.clip(lag, 0, L - 1)], 0.0)
    ng, cg = b_re.shape[0], b_re.shape[2]
    m_op = km.transpose(2, 0, 4, 1, 3).reshape(ng, L * cg, L * cg)
    rr, ri = _cmul(pr[L - 1 - jnp.arange(L)][..., None], pi[L - 1 - jnp.arange(L)][..., None],
                   bbr[None], bbi[None])
    r_op = jnp.concatenate([rr, ri], axis=2).transpose(1, 0, 3, 2).reshape(ng, L * cg, 2 * SSM_STATE)
    orr, oii = _cmul(c_re[None], c_im[None], pr[1:, :, None, :], pi[1:, :, None, :])
    o_op = jnp.concatenate([orr, -oii], axis=3).transpose(1, 3, 0, 2).reshape(ng, 2 * SSM_STATE, L * cg)
    a1 = jnp.concatenate([pr[L], pr[L]], axis=1).reshape(1, -1)
    a2 = jnp.concatenate([-pi[L], pi[L]], axis=1).reshape(1, -1)
    return m_op, r_op, o_op, a1, a2


def _final_kernel(attn_ref, y_ref, u_ref, szs_ref, x_ref, mod_ref, d_ref, gw_ref, gb_ref, wo_ref, o_ref):
    yv = y_ref[0] + d_ref[...] * u_ref[0].astype(F32)
    yg = jax.nn.gelu(yv)
    gl = jax.nn.sigmoid(_dot(yg.astype(BF16), gw_ref[...]) + gb_ref[...])
    ssm = (yg * gl) * szs_ref[0].astype(F32)
    mix = _dot(attn_ref[0], wo_ref[0:ATTN_WIDTH, :]) + _dot(ssm.astype(BF16), wo_ref[ATTN_WIDTH:D_MODEL, :])
    o_ref[0] = x_ref[0] + mod_ref[0, 2:3, :] * mix


def _final(attn, y, u, szs, x, mod3, d_skip, glu_w, glu_b, w_out):
    B, T, D = x.shape
    tm = ROW_TILE
    tok = lambda w: pl.BlockSpec((1, tm, w), lambda b, i: (b, i, 0))
    const = lambda shape: pl.BlockSpec(shape, lambda b, i: tuple(0 for _ in shape))
    return pl.pallas_call(
        _final_kernel,
        grid=(B, T // tm),
        in_specs=[tok(ATTN_WIDTH), tok(SSM_WIDTH), tok(SSM_WIDTH), tok(SSM_WIDTH), tok(D),
                  pl.BlockSpec((1, 3, D), lambda b, i: (b, 0, 0)),
                  const((1, SSM_WIDTH)), const((SSM_WIDTH, SSM_WIDTH)), const((1, SSM_WIDTH)), const((D, D))],
        out_specs=tok(D),
        out_shape=jax.ShapeDtypeStruct((B, T, D), F32),
        compiler_params=_cparams(("parallel", "parallel")),
        name="final",
    )(attn, y, u, szs, x, mod3, d_skip, glu_w, glu_b, w_out)


def _rope_tables(T):
    inv = 1.0 / (ROPE_THETA ** (jnp.arange(0, HEAD_DIM, 2, dtype=F32) / HEAD_DIM))
    ang = jnp.arange(T, dtype=F32)[:, None] * inv[None, :]
    cos, sin = jnp.cos(ang), jnp.sin(ang)
    reps = LANES // HEAD_DIM
    return jnp.tile(jnp.concatenate([cos, cos], axis=1), (1, reps)), jnp.tile(jnp.concatenate([-sin, sin], axis=1), (1, reps))


def _importance_weights(nc):
    r = SEL_BLOCK // CMP_STRIDE
    ov = CMP_LEN // CMP_STRIDE
    w = np.zeros((MAX_SEL_BLOCKS, nc), np.float32)
    for j in range(MAX_SEL_BLOCKS):
        for m in range(r):
            for n in range(ov):
                i = r * j + m - n
                if 0 <= i < nc:
                    w[j, i] += 1.0
    return jnp.asarray(w, BF16)


def _layer(x, c, w_ada, b_ada, norm_g, w_in, q_norm_g, k_cmp_norm_g, k_slc_norm_g, k_win_norm_g,
           cmp_pos_k, cmp_pos_v, cmp_w1_k, cmp_w2_k, cmp_w1_v, cmp_w2_v,
           ssm_a_re, ssm_a_im, ssm_log_dt, ssm_b_re, ssm_b_im, ssm_c_re, ssm_c_im, ssm_d,
           glu_w, glu_b, w_out):
    B, T, D = x.shape
    G = N_KV_GROUPS
    assert D == D_MODEL and T % K_TILE == 0 and T >= WINDOW + Q_TILE and T % ATTN_Q_TILE == 0 and (T // K_TILE) % 2 == 0 and T // SEL_BLOCK <= MAX_SEL_BLOCKS
    assert T // SEL_BLOCK >= SEL_TOPN

    mod3 = _adaln(c, w_ada, b_ada).reshape(B, 3, D)

    o_gbr = ATTN_WIDTH + 6 * KV_WIDTH + ATTN_WIDTH
    o_u = o_gbr + N_BRANCH * N_HEADS
    npg = N_BRANCH * HEADS_PER_GROUP
    gate_cols = [jnp.pad(w_in[:, o_gbr + g * npg:o_gbr + (g + 1) * npg], ((0, 0), (0, LANES - npg))) for g in range(G)]
    w_pad = jnp.concatenate([w_in[:, :o_gbr], w_in[:, o_u:]] + gate_cols, axis=1).astype(BF16)
    assert w_pad.shape[1] == IN_PAD

    cos_t, sin_t = _rope_tables(T)
    hh = np.arange(LANES) // HEAD_DIM
    ones2 = jnp.asarray(hh[:, None] == hh[None, :], BF16)
    tile2 = lambda g: jnp.tile(g, LANES // HEAD_DIM).reshape(1, LANES)
    (q_n, q_r, kcr, vcr, ks_aug, vs_aug, kw, vw_aug, sza, u, szs, gates) = _inproj(
        x, mod3, norm_g.reshape(1, D), w_pad, cos_t, sin_t, ones2,
        tile2(q_norm_g), tile2(k_slc_norm_g), tile2(k_win_norm_g))

    ns = T // CMP_STRIDE
    half = CMP_STRIDE * HEAD_DIM

    def seg_weight(w1_half):
        w4 = w1_half.reshape(CMP_STRIDE, 1, HEAD_DIM, 1, CMP_HIDDEN)
        eye = jnp.eye(G, dtype=F32).reshape(1, G, 1, G, 1)
        return (w4 * eye).reshape(CMP_STRIDE * G * HEAD_DIM, G * CMP_HIDDEN).astype(BF16)

    def out_weight(w2):
        eye = jnp.eye(G, dtype=F32).reshape(G, 1, G, 1)
        return (w2.reshape(1, CMP_HIDDEN, 1, HEAD_DIM) * eye).reshape(G * CMP_HIDDEN, G * HEAD_DIM).astype(BF16)

    pos_rows = lambda p: jnp.broadcast_to(p.reshape(1, CMP_LEN * HEAD_DIM), (8, CMP_LEN * HEAD_DIM))
    kc, vc = _compress(
        kcr.reshape(B, ns, CMP_STRIDE * KV_WIDTH), vcr.reshape(B, ns, CMP_STRIDE * KV_WIDTH),
        seg_weight(cmp_w1_k[:half]), seg_weight(cmp_w1_k[half:]),
        seg_weight(cmp_w1_v[:half]), seg_weight(cmp_w1_v[half:]),
        cmp_w1_k, cmp_w1_v, pos_rows(cmp_pos_k), pos_rows(cmp_pos_v),
        out_weight(cmp_w2_k), out_weight(cmp_w2_v), tile2(k_cmp_norm_g), ones2)

    o_cw, bias = _branches(q_n, q_r, kc, vc, _importance_weights(ns), kw, vw_aug, gates)
    attn = _attn(q_r, bias, ks_aug, vs_aug, o_cw, gates, sza)

    m_op, r_op, o_op, a1, a2 = _s5_operators(ssm_a_re, ssm_a_im, ssm_log_dt, ssm_b_re, ssm_b_im, ssm_c_re, ssm_c_im)
    y = _s5(u, *_split_bf16(r_op), *_split_bf16(m_op), *_split_bf16(o_op), a1, a2)

    return _final(attn, y, u, szs, x, mod3, ssm_d.reshape(1, SSM_WIDTH), glu_w.astype(BF16),
                  glu_b.reshape(1, SSM_WIDTH), w_out.astype(BF16))


def kernel(x, c, w_ada, b_ada, norm_g, w_in, q_norm_g, k_cmp_norm_g, k_slc_norm_g, k_win_norm_g, cmp_pos_k, cmp_pos_v, cmp_w1_k, cmp_w2_k, cmp_w1_v, cmp_w2_v, ssm_a_re, ssm_a_im, ssm_log_dt, ssm_b_re, ssm_b_im, ssm_c_re, ssm_c_im, ssm_d, glu_w, glu_b, w_out):
    params = (w_ada, b_ada, norm_g, w_in, q_norm_g, k_cmp_norm_g, k_slc_norm_g, k_win_norm_g, cmp_pos_k, cmp_pos_v,
              cmp_w1_k, cmp_w2_k, cmp_w1_v, cmp_w2_v, ssm_a_re, ssm_a_im, ssm_log_dt, ssm_b_re, ssm_b_im,
              ssm_c_re, ssm_c_im, ssm_d, glu_w, glu_b, w_out)
    for l in range(w_ada.shape[0]):
        x = _layer(x, c, *(p[l] for p in params))
    return x
```

```python
import functools
import math

import jax
import jax.numpy as jnp
import numpy as np
from jax import lax
from jax.experimental import pallas as pl
from jax.experimental.pallas import tpu as pltpu

F32 = jnp.float32
BF16 = jnp.bfloat16

D_MODEL = 1024
ATTN_WIDTH = 512
N_HEADS = 8
HEAD_DIM = 64
N_KV_GROUPS = 2
HEADS_PER_GROUP = N_HEADS // N_KV_GROUPS
KV_WIDTH = N_KV_GROUPS * HEAD_DIM
CMP_LEN = 32
CMP_STRIDE = 16
CMP_HIDDEN = 256
SEL_BLOCK = 64
SEL_TOPN = 16
WINDOW = 512
N_BRANCH = 3
ROPE_THETA = 10000.0
SSM_WIDTH = D_MODEL - ATTN_WIDTH
SSM_GROUP = 16
SSM_GROUPS = SSM_WIDTH // SSM_GROUP
SSM_STATE = 64
EPS = 1e-6

LANES = 128
MAX_SEL_BLOCKS = LANES
NEG = -1e30
Q_SCALE = HEAD_DIM ** -0.5 * math.log2(math.e)
VMEM_LIMIT = 56 * 1024 * 1024

S5_STEP = 16
ROW_TILE = 512
Q_TILE = 128
ATTN_Q_TILE = 256
K_TILE = 512
BRANCH_TILES = 16

_C_Q, _C_KC, _C_VC, _C_KS, _C_VS, _C_KW, _C_VW = 0, 512, 640, 768, 896, 1024, 1152
_C_ZA, _C_U, _C_ZS, _C_GT = 1280, 1792, 2304, 2816
IN_PAD = _C_GT + N_KV_GROUPS * LANES


def _cparams(sem):
    return pltpu.CompilerParams(dimension_semantics=sem, vmem_limit_bytes=VMEM_LIMIT)


def _split_bf16(a):
    hi = a.astype(BF16)
    lo = (a - hi.astype(F32)).astype(BF16)
    return hi, lo


def _dot(a, b):
    return jnp.dot(a, b, preferred_element_type=F32)


def _dot_nt(a, b):
    return lax.dot_general(a, b, (((1,), (1,)), ((), ())), preferred_element_type=F32)


def _dot_f32(a, b):
    ah, al = _split_bf16(a)
    bh, bl = _split_bf16(b)
    return _dot(ah, bh) + (_dot(al, bh) + _dot(ah, bl))


def _adaln_kernel(c_ref, w_ref, b_ref, o_ref):
    c = c_ref[...]
    o_ref[...] = _dot_f32(jax.nn.silu(c), w_ref[...]) + b_ref[...]


def _adaln(c, w_ada, b_ada):
    B, D = c.shape
    n = w_ada.shape[1]
    tn = 1024
    return pl.pallas_call(
        _adaln_kernel,
        grid=(n // tn,),
        in_specs=[pl.BlockSpec((B, D), lambda j: (0, 0)),
                  pl.BlockSpec((D, tn), lambda j: (0, j)),
                  pl.BlockSpec((1, tn), lambda j: (0, j))],
        out_specs=pl.BlockSpec((B, tn), lambda j: (0, j)),
        out_shape=jax.ShapeDtypeStruct((B, n), F32),
        compiler_params=_cparams(("arbitrary",)),
        name="adaln",
    )(c, w_ada, b_ada.reshape(1, n))


def _head_norm(v, ones_ref, gvec):
    ss = _dot((v * v).astype(BF16), ones_ref[...])
    return v * lax.rsqrt(ss * (1.0 / HEAD_DIM) + EPS) * gvec


def _inproj_kernel(x_ref, mod_ref, ng_ref, w_ref, cos_ref, sin_ref, ones_ref, gq_ref, gks_ref, gkw_ref,
                   qn_ref, qr_ref, kc_ref, vc_ref, ksa_ref, vsa_ref, kw_ref, vwa_ref,
                   sza_ref, u_ref, szs_ref, gt_ref):
    tm = x_ref.shape[1]
    ti = pl.program_id(1)
    x = x_ref[0]
    ms = jnp.mean(x * x, axis=-1, keepdims=True)
    shift = mod_ref[0, 0:1, :]
    scale = mod_ref[0, 1:2, :]
    h = (x * lax.rsqrt(ms + EPS)) * ng_ref[...] * (1.0 + scale) + shift
    hb = h.astype(BF16)

    def proj(c0, width):
        return _dot(hb, w_ref[:, c0:c0 + width])

    cosv = cos_ref[...]
    sinv = sin_ref[...]
    lane = lax.broadcasted_iota(jnp.int32, (tm, LANES), 1)
    row = lax.broadcasted_iota(jnp.int32, (tm, LANES), 0)
    first_half = (lane & (HEAD_DIM - 1)) < (HEAD_DIM // 2)
    low = lane < HEAD_DIM

    def rope(v):
        sw = jnp.where(first_half, pltpu.roll(v, LANES - HEAD_DIM // 2, 1), pltpu.roll(v, HEAD_DIM // 2, 1))
        return v * cosv + sw * sinv

    def group_part(v, g):
        return v if g == 0 else pltpu.roll(v, HEAD_DIM, 1)

    def proj_pair(c0):
        pair = proj(c0, 2 * LANES)
        return pair[:, 0:LANES], pair[:, LANES:2 * LANES]

    for c2 in range(ATTN_WIDTH // (2 * LANES)):
        for c, q in zip((2 * c2, 2 * c2 + 1), proj_pair(_C_Q + 2 * c2 * LANES)):
            qn = _head_norm(q, ones_ref, gq_ref[...]) * Q_SCALE
            qn_ref[0, :, c * LANES:(c + 1) * LANES] = qn.astype(BF16)
            qr_ref[0, :, c * LANES:(c + 1) * LANES] = rope(qn).astype(BF16)

    kcr, vcr = proj_pair(_C_KC)
    kc_ref[0] = kcr.astype(BF16)
    vc_ref[0] = vcr.astype(BF16)

    blk = (ti * tm + row) >> 6
    onehot = jnp.where(lane == blk, 1.0, 0.0).astype(BF16)
    ones_col = jnp.where(lane == HEAD_DIM, 1.0, 0.0)

    ksl, vsl = proj_pair(_C_KS)
    kwn, vwn = proj_pair(_C_KW)
    ksl = rope(_head_norm(ksl, ones_ref, gks_ref[...]))
    kwn = rope(_head_norm(kwn, ones_ref, gkw_ref[...]))
    gate_logits = proj_pair(_C_GT)
    for g in range(N_KV_GROUPS):
        ksa_ref[0, g, :, 0:LANES] = onehot
        ksa_ref[0, g, :, LANES:2 * LANES] = jnp.where(low, group_part(ksl, g), 0.0).astype(BF16)
        vsa_ref[0, g] = jnp.where(low, group_part(vsl, g), ones_col).astype(BF16)
        kw_ref[0, g] = jnp.where(low, group_part(kwn, g), 0.0).astype(BF16)
        vwa_ref[0, g] = jnp.where(low, group_part(vwn, g), ones_col).astype(BF16)
        gt_ref[0, g] = jax.nn.sigmoid(gate_logits[g])

    sza_ref[0] = jax.nn.silu(proj(_C_ZA, ATTN_WIDTH)).astype(BF16)
    u_ref[0] = proj(_C_U, SSM_WIDTH).astype(BF16)
    szs_ref[0] = jax.nn.silu(proj(_C_ZS, SSM_WIDTH)).astype(BF16)


def _inproj(x, mod3, norm_g, w_pad, cos_t, sin_t, ones2, gq, gks, gkw):
    B, T, D = x.shape
    tm = ROW_TILE
    G = N_KV_GROUPS
    tok = lambda w: pl.BlockSpec((1, tm, w), lambda b, i: (b, i, 0))
    grp = lambda w: pl.BlockSpec((1, G, tm, w), lambda b, i: (b, 0, i, 0))
    const = lambda shape: pl.BlockSpec(shape, lambda b, i: tuple(0 for _ in shape))
    tshape = lambda w, dt: jax.ShapeDtypeStruct((B, T, w), dt)
    gshape = lambda w, dt: jax.ShapeDtypeStruct((B, G, T, w), dt)
    return pl.pallas_call(
        _inproj_kernel,
        grid=(B, T // tm),
        in_specs=[tok(D),
                  pl.BlockSpec((1, 3, D), lambda b, i: (b, 0, 0)),
                  const((1, D)),
                  const((D, IN_PAD)),
                  pl.BlockSpec((tm, LANES), lambda b, i: (i, 0)),
                  pl.BlockSpec((tm, LANES), lambda b, i: (i, 0)),
                  const((LANES, LANES)), const((1, LANES)), const((1, LANES)), const((1, LANES))],
        out_specs=[tok(ATTN_WIDTH), tok(ATTN_WIDTH), tok(LANES), tok(LANES),
                   grp(2 * LANES), grp(LANES), grp(LANES), grp(LANES),
                   tok(ATTN_WIDTH), tok(SSM_WIDTH), tok(SSM_WIDTH), grp(LANES)],
        out_shape=[tshape(ATTN_WIDTH, BF16), tshape(ATTN_WIDTH, BF16), tshape(LANES, BF16), tshape(LANES, BF16),
                   gshape(2 * LANES, BF16), gshape(LANES, BF16), gshape(LANES, BF16), gshape(LANES, BF16),
                   tshape(ATTN_WIDTH, BF16), tshape(SSM_WIDTH, BF16), tshape(SSM_WIDTH, BF16), gshape(LANES, F32)],
        compiler_params=_cparams(("parallel", "parallel")),
        name="inproj",
    )(x, mod3, norm_g, w_pad, cos_t, sin_t, ones2, gq, gks, gkw)


def _compress_kernel(kx_ref, vx_ref, wtk_ref, wbk_ref, wtv_ref, wbv_ref, w1k_ref, w1v_ref, posk_ref, posv_ref,
                     w2k_ref, w2v_ref, gk_ref, ones_ref, kc_ref, vc_ref):
    ns = kx_ref.shape[1]
    lane = lax.broadcasted_iota(jnp.int32, (ns, LANES), 1)
    row = lax.broadcasted_iota(jnp.int32, (ns, LANES), 0)
    low = lane < HEAD_DIM
    live = row < ns - 1

    def mlp(x_ref, wt_ref, wb_ref, w1_ref, pos_ref, w2_ref):
        x = x_ref[0]
        top = _dot(x, wt_ref[...])
        bot = pltpu.roll(_dot(x, wb_ref[...]), ns - 1, 0)
        b1 = _dot_f32(pos_ref[...], w1_ref[...])[0:1]
        hid = top + bot + jnp.concatenate([b1, b1], axis=1)
        return _dot(jax.nn.gelu(hid).astype(BF16), w2_ref[...])

    kc = mlp(kx_ref, wtk_ref, wbk_ref, w1k_ref, posk_ref, w2k_ref)
    kc = _head_norm(kc, ones_ref, gk_ref[...])
    vc = mlp(vx_ref, wtv_ref, wbv_ref, w1v_ref, posv_ref, w2v_ref)
    for g in range(N_KV_GROUPS):
        kg = kc if g == 0 else pltpu.roll(kc, HEAD_DIM, 1)
        vg = vc if g == 0 else pltpu.roll(vc, HEAD_DIM, 1)
        kc_ref[0, g] = jnp.where(low & live, kg, 0.0).astype(BF16)
        vc_ref[0, g] = jnp.where(low & live, vg, 0.0).astype(BF16)


def _compress(kx, vx, wtk, wbk, wtv, wbv, w1k, w1v, posk, posv, w2k, w2v, gk, ones2):
    B, ns, w = kx.shape
    G = N_KV_GROUPS
    const = lambda a: pl.BlockSpec(a.shape, lambda b: tuple(0 for _ in a.shape))
    consts = (wtk, wbk, wtv, wbv, w1k, w1v, posk, posv, w2k, w2v, gk, ones2)
    return pl.pallas_call(
        _compress_kernel,
        grid=(B,),
        in_specs=[pl.BlockSpec((1, ns, w), lambda b: (b, 0, 0)),
                  pl.BlockSpec((1, ns, w), lambda b: (b, 0, 0))] + [const(a) for a in consts],
        out_specs=[pl.BlockSpec((1, G, ns, LANES), lambda b: (b, 0, 0, 0)),
                   pl.BlockSpec((1, G, ns, LANES), lambda b: (b, 0, 0, 0))],
        out_shape=[jax.ShapeDtypeStruct((B, G, ns, LANES), BF16),
                   jax.ShapeDtypeStruct((B, G, ns, LANES), BF16)],
        compiler_params=_cparams(("parallel",)),
        name="compress",
    )(kx, vx, *consts)


def _stack_heads(q):
    return jnp.concatenate([q[:, h * HEAD_DIM:(h + 1) * HEAD_DIM] for h in range(HEADS_PER_GROUP)], axis=0)


def _branches_kernel(qn_ref, qr_ref, kc_ref, vc_ref, wimp_ref, kw_ref, vw_ref, gt_ref, ocw_ref, bias_ref,
                     sca_ref, scb_ref, swa_ref, swb_ref, *, nsteps):
    tq = Q_TILE
    hg = HEADS_PER_GROUP
    nq = qn_ref.shape[1] // tq
    wk = WINDOW + tq
    first_tile = pl.program_id(2) * nq
    heads = lambda slab: jnp.concatenate([slab] * hg, axis=0)

    def window_start(t0):
        return pl.multiple_of(jnp.maximum(t0 - WINDOW, 0), tq)

    def scores(i, sc_ref, sw_ref, nc):
        off = pl.multiple_of(i * tq, tq)
        t0 = (first_tile + i) * tq
        sc_ref[:, 0:nc] = _dot_nt(_stack_heads(qn_ref[0, pl.ds(off, tq), :]), kc_ref[0, 0, 0:nc, 0:HEAD_DIM])
        sw_ref[...] = _dot_nt(_stack_heads(qr_ref[0, pl.ds(off, tq), :]),
                              kw_ref[0, 0, pl.ds(window_start(t0), wk), 0:HEAD_DIM])

    def finish(i, sc_ref, sw_ref, nc):
        off = pl.multiple_of(i * tq, tq)
        t0 = (first_tile + i) * tq
        r = lax.broadcasted_iota(jnp.int32, (tq, tq), 0)
        c = lax.broadcasted_iota(jnp.int32, (tq, tq), 1)
        zero = jnp.zeros((tq, tq), F32)
        tri_lo = jnp.where(c > r, zero, NEG)
        tri_hi = jnp.where(c <= r, zero, NEG)
        cmp_gap = (lax.broadcasted_iota(jnp.int32, (tq, nc), 1) * CMP_STRIDE + (CMP_LEN - 1)
                   - lax.broadcasted_iota(jnp.int32, (tq, nc), 0))
        r_col = lax.broadcasted_iota(jnp.int32, (hg * tq, 1), 0) & (tq - 1)
        blk = lax.broadcasted_iota(jnp.int32, (MAX_SEL_BLOCKS, tq), 0)
        lane_t = lax.broadcasted_iota(jnp.int32, (MAX_SEL_BLOCKS, tq), 1)

        s = sc_ref[:, 0:nc] + heads(jnp.where(cmp_gap <= t0, 0.0, NEG))
        e = jnp.exp2(s - jnp.max(s, axis=-1, keepdims=True))
        inv = jnp.where(t0 + r_col >= CMP_LEN - 1, 1.0 / jnp.sum(e, axis=-1, keepdims=True), 0.0)
        p = e * inv
        o_cmp = _dot(p.astype(BF16), vc_ref[0, 0, 0:nc, :])

        ps = p[0:tq]
        for h in range(1, hg):
            ps = ps + p[h * tq:(h + 1) * tq]
        p_hi, p_lo = _split_bf16(ps)
        w = wimp_ref[:, 0:nc]
        imp = _dot_nt(w, p_hi) + _dot_nt(w, p_lo)

        cur = (t0 + lane_t) >> 6
        forced = (blk == 0) | (blk == cur) | (blk == cur - 1)
        key = jnp.where(forced, -2.0, jnp.where(blk <= cur, imp, -1.0))
        sel = jnp.zeros((MAX_SEL_BLOCKS, tq), F32)
        for _ in range(SEL_TOPN - 3):
            mx = jnp.max(key, axis=0, keepdims=True)
            first = jnp.min(jnp.where(key == mx, blk, MAX_SEL_BLOCKS), axis=0, keepdims=True)
            hit = blk == first
            sel = jnp.where(hit, 1.0, sel)
            key = jnp.where(hit, -2.0, key)
        bias_t = jnp.where(((sel > 0.5) | forced) & (blk <= cur), 0.0, NEG)
        bias_ref[0, 0, pl.ds(off, tq), :] = jnp.transpose(bias_t).astype(BF16)

        nchunk = wk // tq
        qi = first_tile + i
        full = t0 >= WINDOW
        band = []
        for a in range(nchunk):
            steady = tri_lo if a == 0 else (tri_hi if a == nchunk - 1 else zero)
            clipped = jnp.where(a < qi, zero, jnp.where(a == qi, tri_hi, NEG))
            band.append(jnp.where(full, steady, clipped))
        s = sw_ref[...] + heads(jnp.concatenate(band, axis=1))
        pw = jnp.exp2(s - jnp.max(s, axis=-1, keepdims=True))
        ow = _dot(pw.astype(BF16), vw_ref[0, 0, pl.ds(window_start(t0), wk), :])
        o_win = ow[:, 0:HEAD_DIM] / ow[:, HEAD_DIM:HEAD_DIM + 1]

        gt = gt_ref[0, 0, pl.ds(off, tq), :]
        parts = []
        for h in range(hg):
            g_cmp = gt[:, N_BRANCH * h:N_BRANCH * h + 1]
            g_win = gt[:, N_BRANCH * h + 2:N_BRANCH * h + 3]
            parts.append(g_cmp * o_cmp[h * tq:(h + 1) * tq, 0:HEAD_DIM] + g_win * o_win[h * tq:(h + 1) * tq])
        ocw_ref[0, pl.ds(off, tq), :] = jnp.concatenate(parts, axis=1)

    def run(nc):
        scores(0, sca_ref, swa_ref, nc)

        def body(j, carry):
            i = 2 * j
            scores(i + 1, scb_ref, swb_ref, nc)
            finish(i, sca_ref, swa_ref, nc)
            scores(jnp.minimum(i + 2, nq - 1), sca_ref, swa_ref, nc)
            finish(i + 1, scb_ref, swb_ref, nc)
            return carry

        lax.fori_loop(0, nq // 2, body, 0)

    def visible_cols(step):
        n_vis = ((step + 1) * nq * tq - CMP_LEN) // CMP_STRIDE + 1
        return min(kc_ref.shape[2], -(-n_vis // LANES) * LANES)

    lax.switch(pl.program_id(2), [functools.partial(run, visible_cols(step)) for step in range(nsteps)])


def _branches(q_n, q_r, kc, vc, wimp_t, kw, vw_aug, gates):
    B, T, _ = q_n.shape
    G = N_KV_GROUPS
    nc = kc.shape[2]
    tq = Q_TILE
    tb = min(T, BRANCH_TILES * tq)
    assert T % tb == 0 and (tb // tq) % 2 == 0
    gw = HEADS_PER_GROUP * HEAD_DIM
    rows = HEADS_PER_GROUP * tq
    tokg = lambda: pl.BlockSpec((1, tb, gw), lambda b, g, i: (b, i, g))
    grp = lambda: pl.BlockSpec((1, 1, tb, LANES), lambda b, g, i: (b, g, i, 0))
    res = lambda n: pl.BlockSpec((1, 1, n, LANES), lambda b, g, i: (b, g, 0, 0))
    return pl.pallas_call(
        functools.partial(_branches_kernel, nsteps=T // tb),
        grid=(B, G, T // tb),
        in_specs=[tokg(), tokg(), res(nc), res(nc),
                  pl.BlockSpec((MAX_SEL_BLOCKS, nc), lambda b, g, i: (0, 0)),
                  res(T), res(T), grp()],
        out_specs=[tokg(), grp()],
        out_shape=[jax.ShapeDtypeStruct((B, T, ATTN_WIDTH), F32),
                   jax.ShapeDtypeStruct((B, G, T, LANES), BF16)],
        scratch_shapes=[pltpu.VMEM((rows, nc), F32), pltpu.VMEM((rows, nc), F32),
                        pltpu.VMEM((rows, WINDOW + tq), F32), pltpu.VMEM((rows, WINDOW + tq), F32)],
        compiler_params=_cparams(("parallel", "parallel", "arbitrary")),
        name="branches",
    )(q_n, q_r, kc, vc, wimp_t, kw, vw_aug, gates)


def _attn_kernel(qr_ref, bias_ref, ks_ref, vs_ref, ocw_ref, gt_ref, sza_ref, out_ref,
                 qa_ref, m_ref, acc_ref, sa_ref, sb_ref):
    tq = qr_ref.shape[1]
    hg = HEADS_PER_GROUP
    rows = hg * tq
    tk = K_TILE
    t0 = pl.multiple_of(pl.program_id(2) * tq, tq)
    q = qr_ref[0]
    bias = bias_ref[0, 0]
    zpad = jnp.zeros((tq, LANES - HEAD_DIM), BF16)
    for h in range(hg):
        qa_ref[h * tq:(h + 1) * tq, 0:LANES] = bias
        qa_ref[h * tq:(h + 1) * tq, LANES:2 * LANES] = jnp.concatenate(
            [q[:, h * HEAD_DIM:(h + 1) * HEAD_DIM], zpad], axis=1)

    def qk(kt, dst_ref):
        k0 = pl.multiple_of(kt * tk, tk)
        dst_ref[...] = _dot_nt(qa_ref[...], ks_ref[0, 0, pl.ds(k0, tk), :])

    gap = (lax.broadcasted_iota(jnp.int32, (tq, tk), 1) - lax.broadcasted_iota(jnp.int32, (tq, tk), 0))

    def process(src_ref, kt, causal):
        k0 = pl.multiple_of(kt * tk, tk)
        s = src_ref[...]
        if causal:
            s = s + jnp.concatenate([jnp.where(gap > t0 - k0, NEG, 0.0)] * hg, axis=0)
        m_prev = m_ref[...]
        m_new = jnp.maximum(m_prev, jnp.max(s, axis=-1, keepdims=True))
        alpha = jnp.exp2(m_prev - m_new)
        p = jnp.exp2(s - m_new[:, 0:1])
        acc_ref[...] = acc_ref[...] * alpha + _dot(p.astype(BF16), vs_ref[0, 0, pl.ds(k0, tk), :])
        m_ref[...] = m_new

    n = t0 // tk + 1
    m_ref[...] = jnp.full(m_ref.shape, NEG, F32)
    acc_ref[...] = jnp.zeros(acc_ref.shape, F32)
    qk(0, sa_ref)

    def body(j, carry):
        k0 = 2 * j
        qk(k0 + 1, sb_ref)
        process(sa_ref, k0, False)
        qk(k0 + 2, sa_ref)
        process(sb_ref, k0 + 1, False)
        return carry

    full_pairs = (n - 1) // 2
    lax.fori_loop(0, full_pairs, body, 0)
    k0 = 2 * full_pairs

    def combine():
        acc = acc_ref[...]
        o_slc = acc[:, 0:HEAD_DIM] / acc[:, HEAD_DIM:HEAD_DIM + 1]
        gt = gt_ref[0, 0]
        parts = [gt[:, N_BRANCH * h + 1:N_BRANCH * h + 2] * o_slc[h * tq:(h + 1) * tq] for h in range(hg)]
        attn = (ocw_ref[0] + jnp.concatenate(parts, axis=1)) * sza_ref[0].astype(F32)
        out_ref[0] = attn.astype(BF16)

    def last_single():
        process(sa_ref, k0, True)
        combine()

    def last_pair():
        qk(k0 + 1, sb_ref)
        process(sa_ref, k0, True)
        process(sb_ref, k0 + 1, True)
        combine()

    lax.cond(n % 2 == 1, last_single, last_pair)


def _attn(q_r, bias, ks_aug, vs_aug, o_cw, gates, sza):
    B, T, _ = q_r.shape
    G = N_KV_GROUPS
    tq = ATTN_Q_TILE
    gw = HEADS_PER_GROUP * HEAD_DIM
    rows = HEADS_PER_GROUP * tq
    tokg = lambda: pl.BlockSpec((1, tq, gw), lambda b, g, i: (b, i, g))
    grp = lambda w: pl.BlockSpec((1, 1, tq, w), lambda b, g, i: (b, g, i, 0))
    res = lambda w: pl.BlockSpec((1, 1, T, w), lambda b, g, i: (b, g, 0, 0))
    return pl.pallas_call(
        _attn_kernel,
        grid=(B, G, T // tq),
        in_specs=[tokg(), grp(LANES), res(2 * LANES), res(LANES), tokg(), grp(LANES), tokg()],
        out_specs=tokg(),
        out_shape=jax.ShapeDtypeStruct((B, T, ATTN_WIDTH), BF16),
        scratch_shapes=[pltpu.VMEM((rows, 2 * LANES), BF16),
                        pltpu.VMEM((rows, LANES), F32),
                        pltpu.VMEM((rows, LANES), F32),
                        pltpu.VMEM((rows, K_TILE), F32),
                        pltpu.VMEM((rows, K_TILE), F32)],
        compiler_params=_cparams(("parallel", "parallel", "arbitrary")),
        name="attn",
    )(q_r, bias, ks_aug, vs_aug, o_cw, gates, sza)


def _block_transpose8(xs):
    lane = lax.broadcasted_iota(jnp.int32, xs[0].shape, 1)
    xs = list(xs)
    for d in (4, 2, 1):
        w = d * SSM_GROUP
        keep = (lane & w) == 0
        for i in range(8):
            if i & d:
                continue
            lo, hi = xs[i], xs[i + d]
            xs[i] = jnp.where(keep, lo, pltpu.roll(hi, w, 1))
            xs[i + d] = jnp.where(keep, pltpu.roll(lo, LANES - w, 1), hi)
    return xs


def _s5_fold_kernel(u_ref, rh_ref, rl_ref, ug_ref, z_ref, uf_ref):
    nch = z_ref.shape[0]
    gw = S5_STEP * SSM_GROUP
    sw = 2 * SSM_STATE
    uf_ref[...] = u_ref[0].astype(F32)
    cols = [uf_ref[pl.ds(s, nch, stride=S5_STEP), :] for s in range(S5_STEP)]
    halves = [_block_transpose8(cols[0:8]), _block_transpose8(cols[8:16])]
    for g in range(LANES // SSM_GROUP):
        ug = jnp.concatenate([halves[0][g], halves[1][g]], axis=1).astype(BF16)
        ug_ref[:, g * gw:(g + 1) * gw] = ug
        z_ref[:, g * sw:(g + 1) * sw] = _dot(ug, rh_ref[g]) + _dot(ug, rl_ref[g])


SCAN_WAYS = 4


def _s5_scan_kernel(*refs, batch):
    z_refs, (a1_ref, a2_ref), s_refs = refs[:SCAN_WAYS], refs[SCAN_WAYS:SCAN_WAYS + 2], refs[SCAN_WAYS + 2:]
    nch = z_refs[0].shape[0] // batch
    a1 = [a1_ref[:, i * LANES:(i + 1) * LANES] for i in range(SCAN_WAYS)]
    a2 = [a2_ref[:, i * LANES:(i + 1) * LANES] for i in range(SCAN_WAYS)]

    def step(c, carry):
        out = []
        for i in range(SCAN_WAYS):
            s, t = carry[i]
            s_refs[i][pl.ds(c, batch, stride=nch), :] = s
            z = z_refs[i][pl.ds(c, batch, stride=nch), :]
            out.append((a1[i] * s + a2[i] * t + z, a1[i] * t - a2[i] * s + pltpu.roll(z, SSM_STATE, 1)))
        return tuple(out)

    zero = jnp.zeros((batch, LANES), F32)
    lax.fori_loop(0, nch, step, tuple((zero, zero) for _ in range(SCAN_WAYS)), unroll=8)


def _s5_out_kernel(ug_ref, *refs):
    s_refs, (mh_ref, ml_ref, oh_ref, ol_ref, y_ref) = refs[:SCAN_WAYS], refs[SCAN_WAYS:]
    nch = ug_ref.shape[0]
    gw = S5_STEP * SSM_GROUP
    sw = 2 * SSM_STATE
    ys = []
    for g in range(LANES // SSM_GROUP):
        ug = ug_ref[:, g * gw:(g + 1) * gw]
        way_col = g // SCAN_WAYS
        sh, sl = _split_bf16(s_refs[g % SCAN_WAYS][:, way_col * sw:(way_col + 1) * sw])
        y = _dot(ug, mh_ref[g]) + _dot(ug, ml_ref[g])
        ys.append(y + (_dot(sh, oh_ref[g]) + (_dot(sl, oh_ref[g]) + _dot(sh, ol_ref[g]))))
    for k in range(2):
        cols = _block_transpose8([y[:, k * LANES:(k + 1) * LANES] for y in ys])
        for t8 in range(8):
            y_ref[0, pl.ds(8 * k + t8, nch, stride=S5_STEP), :] = cols[t8]


def _s5(u, r_hi, r_lo, m_hi, m_lo, o_hi, o_lo, a1, a2):
    B, T, _ = u.shape
    ng = SSM_GROUPS
    nch = T // S5_STEP
    gw = S5_STEP * SSM_GROUP
    sw = 2 * SSM_STATE
    gpl = LANES // SSM_GROUP
    nslab = ng // gpl
    ops = lambda r, c: pl.BlockSpec((gpl, r, c), lambda b, j: (j, 0, 0))
    ug, z = pl.pallas_call(
        _s5_fold_kernel,
        grid=(B, nslab),
        in_specs=[pl.BlockSpec((1, T, LANES), lambda b, j: (b, 0, j)), ops(gw, sw), ops(gw, sw)],
        out_specs=[pl.BlockSpec((nch, gpl * gw), lambda b, j: (b, j)),
                   pl.BlockSpec((nch, gpl * sw), lambda b, j: (b, j))],
        out_shape=[jax.ShapeDtypeStruct((B * nch, ng * gw), BF16),
                   jax.ShapeDtypeStruct((B * nch, ng * sw), F32)],
        scratch_shapes=[pltpu.VMEM((T, LANES), F32)],
        compiler_params=_cparams(("parallel", "parallel")),
        name="s5_fold",
    )(u, r_hi, r_lo)
    ways = SCAN_WAYS
    col = lambda i: pl.BlockSpec((B * nch, sw), lambda j: (0, ways * j + i))
    coef = pl.BlockSpec((1, ways * sw), lambda j: (0, j))
    s_ways = pl.pallas_call(
        functools.partial(_s5_scan_kernel, batch=B),
        grid=(ng // ways,),
        in_specs=[col(i) for i in range(ways)] + [coef, coef],
        out_specs=[pl.BlockSpec((B * nch, sw), lambda j: (0, j)) for _ in range(ways)],
        out_shape=[jax.ShapeDtypeStruct((B * nch, ng // ways * sw), F32) for _ in range(ways)],
        compiler_params=_cparams(("parallel",)),
        name="s5_scan",
    )(*([z] * ways), a1, a2)
    per_way = gpl // ways
    return pl.pallas_call(
        _s5_out_kernel,
        grid=(B, nslab),
        in_specs=[pl.BlockSpec((nch, gpl * gw), lambda b, j: (b, j))]
                 + [pl.BlockSpec((nch, per_way * sw), lambda b, j: (b, j)) for _ in range(ways)]
                 + [ops(gw, gw), ops(gw, gw), ops(sw, gw), ops(sw, gw)],
        out_specs=pl.BlockSpec((1, T, LANES), lambda b, j: (b, 0, j)),
        out_shape=jax.ShapeDtypeStruct((B, T, ng * SSM_GROUP), F32),
        compiler_params=_cparams(("parallel", "parallel")),
        name="s5_out",
    )(ug, *s_ways, m_hi, m_lo, o_hi, o_lo)


def _cmul(ar, ai, br, bi):
    return ar * br - ai * bi, ar * bi + ai * br


def _s5_operators(a_re, a_im, log_dt, b_re, b_im, c_re, c_im):
    hp = lax.Precision.HIGHEST
    L = S5_STEP
    dt = jnp.exp(log_dt)[:, None]
    mag = jnp.exp(a_re * dt)
    lr, li = mag * jnp.cos(a_im * dt), mag * jnp.sin(a_im * dt)
    den = a_re * a_re + a_im * a_im
    fr, fi = _cmul(lr - 1.0, li, a_re / den, -a_im / den)
    bbr, bbi = _cmul(fr[..., None], fi[..., None], b_re, b_im)
    pr, pi = [jnp.ones_like(lr)], [jnp.zeros_like(li)]
    for _ in range(L):
        nr, ni = _cmul(pr[-1], pi[-1], lr, li)
        pr.append(nr)
        pi.append(ni)
    pr, pi = jnp.stack(pr), jnp.stack(pi)
    cpr, cpi = _cmul(c_re[None], c_im[None], pr[:L, :, None, :], pi[:L, :, None, :])
    kd = (jnp.einsum('dgcp,gpe->dgce', cpr, bbr, precision=hp)
          - jnp.einsum('dgcp,gpe->dgce', cpi, bbi, precision=hp))
    lag = jnp.arange(L)[None, :] - jnp.arange(L)[:, None]
    km = jnp.where((lag >= 0)[:, :, None, None, None], kd[jnp.clip(lag, 0, L - 1)], 0.0)
    ng, cg = b_re.shape[0], b_re.shape[2]
    m_op = km.transpose(2, 0, 4, 1, 3).reshape(ng, L * cg, L * cg)
    rr, ri = _cmul(pr[L - 1 - jnp.arange(L)][..., None], pi[L - 1 - jnp.arange(L)][..., None],
                   bbr[None], bbi[None])
    r_op = jnp.concatenate([rr, ri], axis=2).transpose(1, 0, 3, 2).reshape(ng, L * cg, 2 * SSM_STATE)
    orr, oii = _cmul(c_re[None], c_im[None], pr[1:, :, None, :], pi[1:, :, None, :])
    o_op = jnp.concatenate([orr, -oii], axis=3).transpose(1, 3, 0, 2).reshape(ng, 2 * SSM_STATE, L * cg)
    a1 = jnp.concatenate([pr[L], pr[L]], axis=1).reshape(1, -1)
    a2 = jnp.concatenate([-pi[L], pi[L]], axis=1).reshape(1, -1)
    return m_op, r_op, o_op, a1, a2


def _final_kernel(attn_ref, y_ref, u_ref, szs_ref, x_ref, mod_ref, d_ref, gw_ref, gb_ref, wo_ref, o_ref):
    yv = y_ref[0] + d_ref[...] * u_ref[0].astype(F32)
    yg = jax.nn.gelu(yv)
    gl = jax.nn.sigmoid(_dot(yg.astype(BF16), gw_ref[...]) + gb_ref[...])
    ssm = (yg * gl) * szs_ref[0].astype(F32)
    mix = _dot(attn_ref[0], wo_ref[0:ATTN_WIDTH, :]) + _dot(ssm.astype(BF16), wo_ref[ATTN_WIDTH:D_MODEL, :])
    o_ref[0] = x_ref[0] + mod_ref[0, 2:3, :] * mix


def _final(attn, y, u, szs, x, mod3, d_skip, glu_w, glu_b, w_out):
    B, T, D = x.shape
    tm = ROW_TILE
    tok = lambda w: pl.BlockSpec((1, tm, w), lambda b, i: (b, i, 0))
    const = lambda shape: pl.BlockSpec(shape, lambda b, i: tuple(0 for _ in shape))
    return pl.pallas_call(
        _final_kernel,
        grid=(B, T // tm),
        in_specs=[tok(ATTN_WIDTH), tok(SSM_WIDTH), tok(SSM_WIDTH), tok(SSM_WIDTH), tok(D),
                  pl.BlockSpec((1, 3, D), lambda b, i: (b, 0, 0)),
                  const((1, SSM_WIDTH)), const((SSM_WIDTH, SSM_WIDTH)), const((1, SSM_WIDTH)), const((D, D))],
        out_specs=tok(D),
        out_shape=jax.ShapeDtypeStruct((B, T, D), F32),
        compiler_params=_cparams(("parallel", "parallel")),
        name="final",
    )(attn, y, u, szs, x, mod3, d_skip, glu_w, glu_b, w_out)


def _rope_tables(T):
    inv = 1.0 / (ROPE_THETA ** (jnp.arange(0, HEAD_DIM, 2, dtype=F32) / HEAD_DIM))
    ang = jnp.arange(T, dtype=F32)[:, None] * inv[None, :]
    cos, sin = jnp.cos(ang), jnp.sin(ang)
    reps = LANES // HEAD_DIM
    return jnp.tile(jnp.concatenate([cos, cos], axis=1), (1, reps)), jnp.tile(jnp.concatenate([-sin, sin], axis=1), (1, reps))


def _importance_weights(nc):
    r = SEL_BLOCK // CMP_STRIDE
    ov = CMP_LEN // CMP_STRIDE
    w = np.zeros((MAX_SEL_BLOCKS, nc), np.float32)
    for j in range(MAX_SEL_BLOCKS):
        for m in range(r):
            for n in range(ov):
                i = r * j + m - n
                if 0 <= i < nc:
                    w[j, i] += 1.0
    return jnp.asarray(w, BF16)


def _layer(x, c, w_ada, b_ada, norm_g, w_in, q_norm_g, k_cmp_norm_g, k_slc_norm_g, k_win_norm_g,
           cmp_pos_k, cmp_pos_v, cmp_w1_k, cmp_w2_k, cmp_w1_v, cmp_w2_v,
           ssm_a_re, ssm_a_im, ssm_log_dt, ssm_b_re, ssm_b_im, ssm_c_re, ssm_c_im, ssm_d,
           glu_w, glu_b, w_out):
    B, T, D = x.shape
    G = N_KV_GROUPS
    assert D == D_MODEL and T % K_TILE == 0 and T >= WINDOW + Q_TILE and T % ATTN_Q_TILE == 0 and T // SEL_BLOCK <= MAX_SEL_BLOCKS
    assert T // SEL_BLOCK >= SEL_TOPN

    mod3 = _adaln(c, w_ada, b_ada).reshape(B, 3, D)

    o_gbr = ATTN_WIDTH + 6 * KV_WIDTH + ATTN_WIDTH
    o_u = o_gbr + N_BRANCH * N_HEADS
    npg = N_BRANCH * HEADS_PER_GROUP
    gate_cols = [jnp.pad(w_in[:, o_gbr + g * npg:o_gbr + (g + 1) * npg], ((0, 0), (0, LANES - npg))) for g in range(G)]
    w_pad = jnp.concatenate([w_in[:, :o_gbr], w_in[:, o_u:]] + gate_cols, axis=1).astype(BF16)
    assert w_pad.shape[1] == IN_PAD

    cos_t, sin_t = _rope_tables(T)
    hh = np.arange(LANES) // HEAD_DIM
    ones2 = jnp.asarray(hh[:, None] == hh[None, :], BF16)
    tile2 = lambda g: jnp.tile(g, LANES // HEAD_DIM).reshape(1, LANES)
    (q_n, q_r, kcr, vcr, ks_aug, vs_aug, kw, vw_aug, sza, u, szs, gates) = _inproj(
        x, mod3, norm_g.reshape(1, D), w_pad, cos_t, sin_t, ones2,
        tile2(q_norm_g), tile2(k_slc_norm_g), tile2(k_win_norm_g))

    ns = T // CMP_STRIDE
    half = CMP_STRIDE * HEAD_DIM

    def seg_weight(w1_half):
        w4 = w1_half.reshape(CMP_STRIDE, 1, HEAD_DIM, 1, CMP_HIDDEN)
        eye = jnp.eye(G, dtype=F32).reshape(1, G, 1, G, 1)
        return (w4 * eye).reshape(CMP_STRIDE * G * HEAD_DIM, G * CMP_HIDDEN).astype(BF16)

    def out_weight(w2):
        eye = jnp.eye(G, dtype=F32).reshape(G, 1, G, 1)
        return (w2.reshape(1, CMP_HIDDEN, 1, HEAD_DIM) * eye).reshape(G * CMP_HIDDEN, G * HEAD_DIM).astype(BF16)

    pos_rows = lambda p: jnp.broadcast_to(p.reshape(1, CMP_LEN * HEAD_DIM), (8, CMP_LEN * HEAD_DIM))
    kc, vc = _compress(
        kcr.reshape(B, ns, CMP_STRIDE * KV_WIDTH), vcr.reshape(B, ns, CMP_STRIDE * KV_WIDTH),
        seg_weight(cmp_w1_k[:half]), seg_weight(cmp_w1_k[half:]),
        seg_weight(cmp_w1_v[:half]), seg_weight(cmp_w1_v[half:]),
        cmp_w1_k, cmp_w1_v, pos_rows(cmp_pos_k), pos_rows(cmp_pos_v),
        out_weight(cmp_w2_k), out_weight(cmp_w2_v), tile2(k_cmp_norm_g), ones2)

    o_cw, bias = _branches(q_n, q_r, kc, vc, _importance_weights(ns), kw, vw_aug, gates)
    attn = _attn(q_r, bias, ks_aug, vs_aug, o_cw, gates, sza)

    m_op, r_op, o_op, a1, a2 = _s5_operators(ssm_a_re, ssm_a_im, ssm_log_dt, ssm_b_re, ssm_b_im, ssm_c_re, ssm_c_im)
    y = _s5(u, *_split_bf16(r_op), *_split_bf16(m_op), *_split_bf16(o_op), a1, a2)

    return _final(attn, y, u, szs, x, mod3, ssm_d.reshape(1, SSM_WIDTH), glu_w.astype(BF16),
                  glu_b.reshape(1, SSM_WIDTH), w_out.astype(BF16))


def kernel(x, c, w_ada, b_ada, norm_g, w_in, q_norm_g, k_cmp_norm_g, k_slc_norm_g, k_win_norm_g, cmp_pos_k, cmp_pos_v, cmp_w1_k, cmp_w2_k, cmp_w1_v, cmp_w2_v, ssm_a_re, ssm_a_im, ssm_log_dt, ssm_b_re, ssm_b_im, ssm_c_re, ssm_c_im, ssm_d, glu_w, glu_b, w_out):
    params = (w_ada, b_ada, norm_g, w_in, q_norm_g, k_cmp_norm_g, k_slc_norm_g, k_win_norm_g, cmp_pos_k, cmp_pos_v,
              cmp_w1_k, cmp_w2_k, cmp_w1_v, cmp_w2_v, ssm_a_re, ssm_a_im, ssm_log_dt, ssm_b_re, ssm_b_im,
              ssm_c_re, ssm_c_im, ssm_d, glu_w, glu_b, w_out)
    for l in range(w_ada.shape[0]):
        x = _layer(x, c, *(p[l] for p in params))
    return x
```

```python
import functools
import math

import jax
import jax.numpy as jnp
import numpy as np
from jax import lax
from jax.experimental import pallas as pl
from jax.experimental.pallas import tpu as pltpu

F32 = jnp.float32
BF16 = jnp.bfloat16

D_MODEL = 1024
ATTN_WIDTH = 512
N_HEADS = 8
HEAD_DIM = 64
N_KV_GROUPS = 2
HEADS_PER_GROUP = N_HEADS // N_KV_GROUPS
KV_WIDTH = N_KV_GROUPS * HEAD_DIM
CMP_LEN = 32
CMP_STRIDE = 16
CMP_HIDDEN = 256
SEL_BLOCK = 64
SEL_TOPN = 16
WINDOW = 512
N_BRANCH = 3
ROPE_THETA = 10000.0
SSM_WIDTH = D_MODEL - ATTN_WIDTH
SSM_GROUP = 16
SSM_GROUPS = SSM_WIDTH // SSM_GROUP
SSM_STATE = 64
EPS = 1e-6

LANES = 128
MAX_SEL_BLOCKS = LANES
NEG = -1e30
Q_SCALE = HEAD_DIM ** -0.5 * math.log2(math.e)
VMEM_LIMIT = 56 * 1024 * 1024

S5_STEP = 16
ROW_TILE = 512
Q_TILE = 128
ATTN_Q_TILE = 512
K_TILE = 512
BRANCH_TILES = 16

_C_Q, _C_KC, _C_VC, _C_KS, _C_VS, _C_KW, _C_VW = 0, 512, 640, 768, 896, 1024, 1152
_C_ZA, _C_U, _C_ZS, _C_GT = 1280, 1792, 2304, 2816
IN_PAD = _C_GT + N_KV_GROUPS * LANES


def _cparams(sem):
    return pltpu.CompilerParams(dimension_semantics=sem, vmem_limit_bytes=VMEM_LIMIT)


def _split_bf16(a):
    hi = a.astype(BF16)
    lo = (a - hi.astype(F32)).astype(BF16)
    return hi, lo


def _dot(a, b):
    return jnp.dot(a, b, preferred_element_type=F32)


def _dot_nt(a, b):
    return lax.dot_general(a, b, (((1,), (1,)), ((), ())), preferred_element_type=F32)


def _dot_f32(a, b):
    ah, al = _split_bf16(a)
    bh, bl = _split_bf16(b)
    return _dot(ah, bh) + (_dot(al, bh) + _dot(ah, bl))


def _adaln_kernel(c_ref, w_ref, b_ref, o_ref):
    c = c_ref[...]
    o_ref[...] = _dot_f32(jax.nn.silu(c), w_ref[...]) + b_ref[...]


def _adaln(c, w_ada, b_ada):
    B, D = c.shape
    n = w_ada.shape[1]
    tn = 1024
    return pl.pallas_call(
        _adaln_kernel,
        grid=(n // tn,),
        in_specs=[pl.BlockSpec((B, D), lambda j: (0, 0)),
                  pl.BlockSpec((D, tn), lambda j: (0, j)),
                  pl.BlockSpec((1, tn), lambda j: (0, j))],
        out_specs=pl.BlockSpec((B, tn), lambda j: (0, j)),
        out_shape=jax.ShapeDtypeStruct((B, n), F32),
        compiler_params=_cparams(("arbitrary",)),
        name="adaln",
    )(c, w_ada, b_ada.reshape(1, n))


def _head_norm(v, ones_ref, gvec):
    ss = _dot((v * v).astype(BF16), ones_ref[...])
    return v * lax.rsqrt(ss * (1.0 / HEAD_DIM) + EPS) * gvec


def _inproj_kernel(x_ref, mod_ref, ng_ref, w_ref, cos_ref, sin_ref, ones_ref, gq_ref, gks_ref, gkw_ref,
                   qn_ref, qr_ref, kc_ref, vc_ref, ksa_ref, vsa_ref, kw_ref, vwa_ref,
                   sza_ref, u_ref, szs_ref, gt_ref):
    tm = x_ref.shape[1]
    ti = pl.program_id(1)
    x = x_ref[0]
    ms = jnp.mean(x * x, axis=-1, keepdims=True)
    shift = mod_ref[0, 0:1, :]
    scale = mod_ref[0, 1:2, :]
    h = (x * lax.rsqrt(ms + EPS)) * ng_ref[...] * (1.0 + scale) + shift
    hb = h.astype(BF16)

    def proj(c0, width):
        return _dot(hb, w_ref[:, c0:c0 + width])

    cosv = cos_ref[...]
    sinv = sin_ref[...]
    lane = lax.broadcasted_iota(jnp.int32, (tm, LANES), 1)
    row = lax.broadcasted_iota(jnp.int32, (tm, LANES), 0)
    first_half = (lane & (HEAD_DIM - 1)) < (HEAD_DIM // 2)
    low = lane < HEAD_DIM

    def rope(v):
        sw = jnp.where(first_half, pltpu.roll(v, LANES - HEAD_DIM // 2, 1), pltpu.roll(v, HEAD_DIM // 2, 1))
        return v * cosv + sw * sinv

    def group_part(v, g):
        return v if g == 0 else pltpu.roll(v, HEAD_DIM, 1)

    def proj_pair(c0):
        pair = proj(c0, 2 * LANES)
        return pair[:, 0:LANES], pair[:, LANES:2 * LANES]

    for c2 in range(ATTN_WIDTH // (2 * LANES)):
        for c, q in zip((2 * c2, 2 * c2 + 1), proj_pair(_C_Q + 2 * c2 * LANES)):
            qn = _head_norm(q, ones_ref, gq_ref[...]) * Q_SCALE
            qn_ref[0, :, c * LANES:(c + 1) * LANES] = qn.astype(BF16)
            qr_ref[0, :, c * LANES:(c + 1) * LANES] = rope(qn).astype(BF16)

    kcr, vcr = proj_pair(_C_KC)
    kc_ref[0] = kcr.astype(BF16)
    vc_ref[0] = vcr.astype(BF16)

    blk = (ti * tm + row) >> 6
    onehot = jnp.where(lane == blk, 1.0, 0.0).astype(BF16)
    ones_col = jnp.where(lane == HEAD_DIM, 1.0, 0.0)

    ksl, vsl = proj_pair(_C_KS)
    kwn, vwn = proj_pair(_C_KW)
    ksl = rope(_head_norm(ksl, ones_ref, gks_ref[...]))
    kwn = rope(_head_norm(kwn, ones_ref, gkw_ref[...]))
    gate_logits = proj_pair(_C_GT)
    for g in range(N_KV_GROUPS):
        ksa_ref[0, g, :, 0:LANES] = onehot
        ksa_ref[0, g, :, LANES:2 * LANES] = jnp.where(low, group_part(ksl, g), 0.0).astype(BF16)
        vsa_ref[0, g] = jnp.where(low, group_part(vsl, g), ones_col).astype(BF16)
        kw_ref[0, g] = jnp.where(low, group_part(kwn, g), 0.0).astype(BF16)
        vwa_ref[0, g] = jnp.where(low, group_part(vwn, g), ones_col).astype(BF16)
        gt_ref[0, g] = jax.nn.sigmoid(gate_logits[g])

    sza_ref[0] = jax.nn.silu(proj(_C_ZA, ATTN_WIDTH)).astype(BF16)
    u_ref[0] = proj(_C_U, SSM_WIDTH).astype(BF16)
    szs_ref[0] = jax.nn.silu(proj(_C_ZS, SSM_WIDTH)).astype(BF16)


def _inproj(x, mod3, norm_g, w_pad, cos_t, sin_t, ones2, gq, gks, gkw):
    B, T, D = x.shape
    tm = ROW_TILE
    G = N_KV_GROUPS
    tok = lambda w: pl.BlockSpec((1, tm, w), lambda b, i: (b, i, 0))
    grp = lambda w: pl.BlockSpec((1, G, tm, w), lambda b, i: (b, 0, i, 0))
    const = lambda shape: pl.BlockSpec(shape, lambda b, i: tuple(0 for _ in shape))
    tshape = lambda w, dt: jax.ShapeDtypeStruct((B, T, w), dt)
    gshape = lambda w, dt: jax.ShapeDtypeStruct((B, G, T, w), dt)
    return pl.pallas_call(
        _inproj_kernel,
        grid=(B, T // tm),
        in_specs=[tok(D),
                  pl.BlockSpec((1, 3, D), lambda b, i: (b, 0, 0)),
                  const((1, D)),
                  const((D, IN_PAD)),
                  pl.BlockSpec((tm, LANES), lambda b, i: (i, 0)),
                  pl.BlockSpec((tm, LANES), lambda b, i: (i, 0)),
                  const((LANES, LANES)), const((1, LANES)), const((1, LANES)), const((1, LANES))],
        out_specs=[tok(ATTN_WIDTH), tok(ATTN_WIDTH), tok(LANES), tok(LANES),
                   grp(2 * LANES), grp(LANES), grp(LANES), grp(LANES),
                   tok(ATTN_WIDTH), tok(SSM_WIDTH), tok(SSM_WIDTH), grp(LANES)],
        out_shape=[tshape(ATTN_WIDTH, BF16), tshape(ATTN_WIDTH, BF16), tshape(LANES, BF16), tshape(LANES, BF16),
                   gshape(2 * LANES, BF16), gshape(LANES, BF16), gshape(LANES, BF16), gshape(LANES, BF16),
                   tshape(ATTN_WIDTH, BF16), tshape(SSM_WIDTH, BF16), tshape(SSM_WIDTH, BF16), gshape(LANES, F32)],
        compiler_params=_cparams(("parallel", "parallel")),
        name="inproj",
    )(x, mod3, norm_g, w_pad, cos_t, sin_t, ones2, gq, gks, gkw)


def _compress_kernel(kx_ref, vx_ref, wtk_ref, wbk_ref, wtv_ref, wbv_ref, w1k_ref, w1v_ref, posk_ref, posv_ref,
                     w2k_ref, w2v_ref, gk_ref, ones_ref, kc_ref, vc_ref):
    ns = kx_ref.shape[1]
    lane = lax.broadcasted_iota(jnp.int32, (ns, LANES), 1)
    row = lax.broadcasted_iota(jnp.int32, (ns, LANES), 0)
    low = lane < HEAD_DIM
    live = row < ns - 1

    def mlp(x_ref, wt_ref, wb_ref, w1_ref, pos_ref, w2_ref):
        x = x_ref[0]
        top = _dot(x, wt_ref[...])
        bot = pltpu.roll(_dot(x, wb_ref[...]), ns - 1, 0)
        b1 = _dot_f32(pos_ref[...], w1_ref[...])[0:1]
        hid = top + bot + jnp.concatenate([b1, b1], axis=1)
        return _dot(jax.nn.gelu(hid).astype(BF16), w2_ref[...])

    kc = mlp(kx_ref, wtk_ref, wbk_ref, w1k_ref, posk_ref, w2k_ref)
    kc = _head_norm(kc, ones_ref, gk_ref[...])
    vc = mlp(vx_ref, wtv_ref, wbv_ref, w1v_ref, posv_ref, w2v_ref)
    for g in range(N_KV_GROUPS):
        kg = kc if g == 0 else pltpu.roll(kc, HEAD_DIM, 1)
        vg = vc if g == 0 else pltpu.roll(vc, HEAD_DIM, 1)
        kc_ref[0, g] = jnp.where(low & live, kg, 0.0).astype(BF16)
        vc_ref[0, g] = jnp.where(low & live, vg, 0.0).astype(BF16)


def _compress(kx, vx, wtk, wbk, wtv, wbv, w1k, w1v, posk, posv, w2k, w2v, gk, ones2):
    B, ns, w = kx.shape
    G = N_KV_GROUPS
    const = lambda a: pl.BlockSpec(a.shape, lambda b: tuple(0 for _ in a.shape))
    consts = (wtk, wbk, wtv, wbv, w1k, w1v, posk, posv, w2k, w2v, gk, ones2)
    return pl.pallas_call(
        _compress_kernel,
        grid=(B,),
        in_specs=[pl.BlockSpec((1, ns, w), lambda b: (b, 0, 0)),
                  pl.BlockSpec((1, ns, w), lambda b: (b, 0, 0))] + [const(a) for a in consts],
        out_specs=[pl.BlockSpec((1, G, ns, LANES), lambda b: (b, 0, 0, 0)),
                   pl.BlockSpec((1, G, ns, LANES), lambda b: (b, 0, 0, 0))],
        out_shape=[jax.ShapeDtypeStruct((B, G, ns, LANES), BF16),
                   jax.ShapeDtypeStruct((B, G, ns, LANES), BF16)],
        compiler_params=_cparams(("parallel",)),
        name="compress",
    )(kx, vx, *consts)


def _stack_heads(q):
    return jnp.concatenate([q[:, h * HEAD_DIM:(h + 1) * HEAD_DIM] for h in range(HEADS_PER_GROUP)], axis=0)


def _branches_kernel(qn_ref, qr_ref, kc_ref, vc_ref, wimp_ref, kw_ref, vw_ref, gt_ref, ocw_ref, bias_ref,
                     sca_ref, scb_ref, swa_ref, swb_ref, *, nsteps):
    tq = Q_TILE
    hg = HEADS_PER_GROUP
    nq = qn_ref.shape[1] // tq
    wk = WINDOW + tq
    first_tile = pl.program_id(2) * nq
    heads = lambda slab: jnp.concatenate([slab] * hg, axis=0)

    def window_start(t0):
        return pl.multiple_of(jnp.maximum(t0 - WINDOW, 0), tq)

    def scores(i, sc_ref, sw_ref, nc):
        off = pl.multiple_of(i * tq, tq)
        t0 = (first_tile + i) * tq
        sc_ref[:, 0:nc] = _dot_nt(_stack_heads(qn_ref[0, pl.ds(off, tq), :]), kc_ref[0, 0, 0:nc, 0:HEAD_DIM])
        sw_ref[...] = _dot_nt(_stack_heads(qr_ref[0, pl.ds(off, tq), :]),
                              kw_ref[0, 0, pl.ds(window_start(t0), wk), 0:HEAD_DIM])

    def finish(i, sc_ref, sw_ref, nc):
        off = pl.multiple_of(i * tq, tq)
        t0 = (first_tile + i) * tq
        r = lax.broadcasted_iota(jnp.int32, (tq, tq), 0)
        c = lax.broadcasted_iota(jnp.int32, (tq, tq), 1)
        zero = jnp.zeros((tq, tq), F32)
        tri_lo = jnp.where(c > r, zero, NEG)
        tri_hi = jnp.where(c <= r, zero, NEG)
        cmp_gap = (lax.broadcasted_iota(jnp.int32, (tq, nc), 1) * CMP_STRIDE + (CMP_LEN - 1)
                   - lax.broadcasted_iota(jnp.int32, (tq, nc), 0))
        r_col = lax.broadcasted_iota(jnp.int32, (hg * tq, 1), 0) & (tq - 1)
        blk = lax.broadcasted_iota(jnp.int32, (MAX_SEL_BLOCKS, tq), 0)
        lane_t = lax.broadcasted_iota(jnp.int32, (MAX_SEL_BLOCKS, tq), 1)

        s = sc_ref[:, 0:nc] + heads(jnp.where(cmp_gap <= t0, 0.0, NEG))
        e = jnp.exp2(s - jnp.max(s, axis=-1, keepdims=True))
        inv = jnp.where(t0 + r_col >= CMP_LEN - 1, 1.0 / jnp.sum(e, axis=-1, keepdims=True), 0.0)
        p = e * inv
        o_cmp = _dot(p.astype(BF16), vc_ref[0, 0, 0:nc, :])

        ps = p[0:tq]
        for h in range(1, hg):
            ps = ps + p[h * tq:(h + 1) * tq]
        p_hi, p_lo = _split_bf16(ps)
        w = wimp_ref[:, 0:nc]
        imp = _dot_nt(w, p_hi) + _dot_nt(w, p_lo)

        cur = (t0 + lane_t) >> 6
        forced = (blk == 0) | (blk == cur) | (blk == cur - 1)
        key = jnp.where(forced, -2.0, jnp.where(blk <= cur, imp, -1.0))
        sel = jnp.zeros((MAX_SEL_BLOCKS, tq), F32)
        for _ in range(SEL_TOPN - 3):
            mx = jnp.max(key, axis=0, keepdims=True)
            first = jnp.min(jnp.where(key == mx, blk, MAX_SEL_BLOCKS), axis=0, keepdims=True)
            hit = blk == first
            sel = jnp.where(hit, 1.0, sel)
            key = jnp.where(hit, -2.0, key)
        bias_t = jnp.where(((sel > 0.5) | forced) & (blk <= cur), 0.0, NEG)
        bias_ref[0, 0, pl.ds(off, tq), :] = jnp.transpose(bias_t).astype(BF16)

        nchunk = wk // tq
        qi = first_tile + i
        full = t0 >= WINDOW
        band = []
        for a in range(nchunk):
            steady = tri_lo if a == 0 else (tri_hi if a == nchunk - 1 else zero)
            clipped = jnp.where(a < qi, zero, jnp.where(a == qi, tri_hi, NEG))
            band.append(jnp.where(full, steady, clipped))
        s = sw_ref[...] + heads(jnp.concatenate(band, axis=1))
        pw = jnp.exp2(s - jnp.max(s, axis=-1, keepdims=True))
        ow = _dot(pw.astype(BF16), vw_ref[0, 0, pl.ds(window_start(t0), wk), :])
        o_win = ow[:, 0:HEAD_DIM] / ow[:, HEAD_DIM:HEAD_DIM + 1]

        gt = gt_ref[0, 0, pl.ds(off, tq), :]
        parts = []
        for h in range(hg):
            g_cmp = gt[:, N_BRANCH * h:N_BRANCH * h + 1]
            g_win = gt[:, N_BRANCH * h + 2:N_BRANCH * h + 3]
            parts.append(g_cmp * o_cmp[h * tq:(h + 1) * tq, 0:HEAD_DIM] + g_win * o_win[h * tq:(h + 1) * tq])
        ocw_ref[0, pl.ds(off, tq), :] = jnp.concatenate(parts, axis=1)

    def run(nc):
        scores(0, sca_ref, swa_ref, nc)

        def body(j, carry):
            i = 2 * j
            scores(i + 1, scb_ref, swb_ref, nc)
            finish(i, sca_ref, swa_ref, nc)
            scores(jnp.minimum(i + 2, nq - 1), sca_ref, swa_ref, nc)
            finish(i + 1, scb_ref, swb_ref, nc)
            return carry

        lax.fori_loop(0, nq // 2, body, 0)

    def visible_cols(step):
        n_vis = ((step + 1) * nq * tq - CMP_LEN) // CMP_STRIDE + 1
        return min(kc_ref.shape[2], -(-n_vis // LANES) * LANES)

    lax.switch(pl.program_id(2), [functools.partial(run, visible_cols(step)) for step in range(nsteps)])


def _branches(q_n, q_r, kc, vc, wimp_t, kw, vw_aug, gates):
    B, T, _ = q_n.shape
    G = N_KV_GROUPS
    nc = kc.shape[2]
    tq = Q_TILE
    tb = min(T, BRANCH_TILES * tq)
    assert T % tb == 0 and (tb // tq) % 2 == 0
    gw = HEADS_PER_GROUP * HEAD_DIM
    rows = HEADS_PER_GROUP * tq
    tokg = lambda: pl.BlockSpec((1, tb, gw), lambda b, g, i: (b, i, g))
    grp = lambda: pl.BlockSpec((1, 1, tb, LANES), lambda b, g, i: (b, g, i, 0))
    res = lambda n: pl.BlockSpec((1, 1, n, LANES), lambda b, g, i: (b, g, 0, 0))
    return pl.pallas_call(
        functools.partial(_branches_kernel, nsteps=T // tb),
        grid=(B, G, T // tb),
        in_specs=[tokg(), tokg(), res(nc), res(nc),
                  pl.BlockSpec((MAX_SEL_BLOCKS, nc), lambda b, g, i: (0, 0)),
                  res(T), res(T), grp()],
        out_specs=[tokg(), grp()],
        out_shape=[jax.ShapeDtypeStruct((B, T, ATTN_WIDTH), F32),
                   jax.ShapeDtypeStruct((B, G, T, LANES), BF16)],
        scratch_shapes=[pltpu.VMEM((rows, nc), F32), pltpu.VMEM((rows, nc), F32),
                        pltpu.VMEM((rows, WINDOW + tq), F32), pltpu.VMEM((rows, WINDOW + tq), F32)],
        compiler_params=_cparams(("parallel", "parallel", "arbitrary")),
        name="branches",
    )(q_n, q_r, kc, vc, wimp_t, kw, vw_aug, gates)


def _attn_kernel(qr_ref, bias_ref, ks_ref, vs_ref, ocw_ref, gt_ref, sza_ref, out_ref,
                 qa_ref, m_ref, acc_ref, sa_ref, sb_ref):
    tq = qr_ref.shape[1]
    hg = HEADS_PER_GROUP
    rows = hg * tq
    tk = K_TILE
    t0 = pl.multiple_of(pl.program_id(2) * tq, tq)
    q = qr_ref[0]
    bias = bias_ref[0, 0]
    zpad = jnp.zeros((tq, LANES - HEAD_DIM), BF16)
    for h in range(hg):
        qa_ref[h * tq:(h + 1) * tq, 0:LANES] = bias
        qa_ref[h * tq:(h + 1) * tq, LANES:2 * LANES] = jnp.concatenate(
            [q[:, h * HEAD_DIM:(h + 1) * HEAD_DIM], zpad], axis=1)

    def qk(kt, dst_ref):
        k0 = pl.multiple_of(kt * tk, tk)
        dst_ref[...] = _dot_nt(qa_ref[...], ks_ref[0, 0, pl.ds(k0, tk), :])

    gap = (lax.broadcasted_iota(jnp.int32, (tq, tk), 1) - lax.broadcasted_iota(jnp.int32, (tq, tk), 0))

    def process(src_ref, kt, causal):
        k0 = pl.multiple_of(kt * tk, tk)
        s = src_ref[...]
        if causal:
            s = s + jnp.concatenate([jnp.where(gap > t0 - k0, NEG, 0.0)] * hg, axis=0)
        m_prev = m_ref[...]
        m_new = jnp.maximum(m_prev, jnp.max(s, axis=-1, keepdims=True))
        alpha = jnp.exp2(m_prev - m_new)
        p = jnp.exp2(s - m_new[:, 0:1])
        acc_ref[...] = acc_ref[...] * alpha + _dot(p.astype(BF16), vs_ref[0, 0, pl.ds(k0, tk), :])
        m_ref[...] = m_new

    n = t0 // tk + 1
    m_ref[...] = jnp.full(m_ref.shape, NEG, F32)
    acc_ref[...] = jnp.zeros(acc_ref.shape, F32)
    qk(0, sa_ref)

    def body(j, carry):
        k0 = 2 * j
        qk(k0 + 1, sb_ref)
        process(sa_ref, k0, False)
        qk(k0 + 2, sa_ref)
        process(sb_ref, k0 + 1, False)
        return carry

    full_pairs = (n - 1) // 2
    lax.fori_loop(0, full_pairs, body, 0)
    k0 = 2 * full_pairs

    def combine():
        acc = acc_ref[...]
        o_slc = acc[:, 0:HEAD_DIM] / acc[:, HEAD_DIM:HEAD_DIM + 1]
        gt = gt_ref[0, 0]
        parts = [gt[:, N_BRANCH * h + 1:N_BRANCH * h + 2] * o_slc[h * tq:(h + 1) * tq] for h in range(hg)]
        attn = (ocw_ref[0] + jnp.concatenate(parts, axis=1)) * sza_ref[0].astype(F32)
        out_ref[0] = attn.astype(BF16)

    def last_single():
        process(sa_ref, k0, True)
        combine()

    def last_pair():
        qk(k0 + 1, sb_ref)
        process(sa_ref, k0, True)
        process(sb_ref, k0 + 1, True)
        combine()

    lax.cond(n % 2 == 1, last_single, last_pair)


def _attn(q_r, bias, ks_aug, vs_aug, o_cw, gates, sza):
    B, T, _ = q_r.shape
    G = N_KV_GROUPS
    tq = ATTN_Q_TILE
    gw = HEADS_PER_GROUP * HEAD_DIM
    rows = HEADS_PER_GROUP * tq
    tokg = lambda: pl.BlockSpec((1, tq, gw), lambda b, g, i: (b, i, g))
    grp = lambda w: pl.BlockSpec((1, 1, tq, w), lambda b, g, i: (b, g, i, 0))
    res = lambda w: pl.BlockSpec((1, 1, T, w), lambda b, g, i: (b, g, 0, 0))
    return pl.pallas_call(
        _attn_kernel,
        grid=(B, G, T // tq),
        in_specs=[tokg(), grp(LANES), res(2 * LANES), res(LANES), tokg(), grp(LANES), tokg()],
        out_specs=tokg(),
        out_shape=jax.ShapeDtypeStruct((B, T, ATTN_WIDTH), BF16),
        scratch_shapes=[pltpu.VMEM((rows, 2 * LANES), BF16),
                        pltpu.VMEM((rows, LANES), F32),
                        pltpu.VMEM((rows, LANES), F32),
                        pltpu.VMEM((rows, K_TILE), F32),
                        pltpu.VMEM((rows, K_TILE), F32)],
        compiler_params=_cparams(("parallel", "parallel", "arbitrary")),
        name="attn",
    )(q_r, bias, ks_aug, vs_aug, o_cw, gates, sza)


def _block_transpose8(xs):
    lane = lax.broadcasted_iota(jnp.int32, xs[0].shape, 1)
    xs = list(xs)
    for d in (4, 2, 1):
        w = d * SSM_GROUP
        keep = (lane & w) == 0
        for i in range(8):
            if i & d:
                continue
            lo, hi = xs[i], xs[i + d]
            xs[i] = jnp.where(keep, lo, pltpu.roll(hi, w, 1))
            xs[i + d] = jnp.where(keep, pltpu.roll(lo, LANES - w, 1), hi)
    return xs


def _s5_fold_kernel(u_ref, rh_ref, rl_ref, ug_ref, z_ref, uf_ref):
    nch = z_ref.shape[0]
    gw = S5_STEP * SSM_GROUP
    sw = 2 * SSM_STATE
    uf_ref[...] = u_ref[0].astype(F32)
    cols = [uf_ref[pl.ds(s, nch, stride=S5_STEP), :] for s in range(S5_STEP)]
    halves = [_block_transpose8(cols[0:8]), _block_transpose8(cols[8:16])]
    for g in range(LANES // SSM_GROUP):
        ug = jnp.concatenate([halves[0][g], halves[1][g]], axis=1).astype(BF16)
        ug_ref[:, g * gw:(g + 1) * gw] = ug
        z_ref[:, g * sw:(g + 1) * sw] = _dot(ug, rh_ref[g]) + _dot(ug, rl_ref[g])


SCAN_WAYS = 4


def _s5_scan_kernel(*refs, batch):
    z_refs, (a1_ref, a2_ref), s_refs = refs[:SCAN_WAYS], refs[SCAN_WAYS:SCAN_WAYS + 2], refs[SCAN_WAYS + 2:]
    nch = z_refs[0].shape[0] // batch
    a1 = [a1_ref[:, i * LANES:(i + 1) * LANES] for i in range(SCAN_WAYS)]
    a2 = [a2_ref[:, i * LANES:(i + 1) * LANES] for i in range(SCAN_WAYS)]

    def step(c, carry):
        out = []
        for i in range(SCAN_WAYS):
            s, t = carry[i]
            s_refs[i][pl.ds(c, batch, stride=nch), :] = s
            z = z_refs[i][pl.ds(c, batch, stride=nch), :]
            out.append((a1[i] * s + a2[i] * t + z, a1[i] * t - a2[i] * s + pltpu.roll(z, SSM_STATE, 1)))
        return tuple(out)

    zero = jnp.zeros((batch, LANES), F32)
    lax.fori_loop(0, nch, step, tuple((zero, zero) for _ in range(SCAN_WAYS)), unroll=8)


def _s5_out_kernel(ug_ref, *refs):
    s_refs, (mh_ref, ml_ref, oh_ref, ol_ref, y_ref) = refs[:SCAN_WAYS], refs[SCAN_WAYS:]
    nch = ug_ref.shape[0]
    gw = S5_STEP * SSM_GROUP
    sw = 2 * SSM_STATE
    ys = []
    for g in range(LANES // SSM_GROUP):
        ug = ug_ref[:, g * gw:(g + 1) * gw]
        way_col = g // SCAN_WAYS
        sh, sl = _split_bf16(s_refs[g % SCAN_WAYS][:, way_col * sw:(way_col + 1) * sw])
        y = _dot(ug, mh_ref[g]) + _dot(ug, ml_ref[g])
        ys.append(y + (_dot(sh, oh_ref[g]) + (_dot(sl, oh_ref[g]) + _dot(sh, ol_ref[g]))))
    for k in range(2):
        cols = _block_transpose8([y[:, k * LANES:(k + 1) * LANES] for y in ys])
        for t8 in range(8):
            y_ref[0, pl.ds(8 * k + t8, nch, stride=S5_STEP), :] = cols[t8]


def _s5(u, r_hi, r_lo, m_hi, m_lo, o_hi, o_lo, a1, a2):
    B, T, _ = u.shape
    ng = SSM_GROUPS
    nch = T // S5_STEP
    gw = S5_STEP * SSM_GROUP
    sw = 2 * SSM_STATE
    gpl = LANES // SSM_GROUP
    nslab = ng // gpl
    ops = lambda r, c: pl.BlockSpec((gpl, r, c), lambda b, j: (j, 0, 0))
    ug, z = pl.pallas_call(
        _s5_fold_kernel,
        grid=(B, nslab),
        in_specs=[pl.BlockSpec((1, T, LANES), lambda b, j: (b, 0, j)), ops(gw, sw), ops(gw, sw)],
        out_specs=[pl.BlockSpec((nch, gpl * gw), lambda b, j: (b, j)),
                   pl.BlockSpec((nch, gpl * sw), lambda b, j: (b, j))],
        out_shape=[jax.ShapeDtypeStruct((B * nch, ng * gw), BF16),
                   jax.ShapeDtypeStruct((B * nch, ng * sw), F32)],
        scratch_shapes=[pltpu.VMEM((T, LANES), F32)],
        compiler_params=_cparams(("parallel", "parallel")),
        name="s5_fold",
    )(u, r_hi, r_lo)
    ways = SCAN_WAYS
    col = lambda i: pl.BlockSpec((B * nch, sw), lambda j: (0, ways * j + i))
    coef = pl.BlockSpec((1, ways * sw), lambda j: (0, j))
    s_ways = pl.pallas_call(
        functools.partial(_s5_scan_kernel, batch=B),
        grid=(ng // ways,),
        in_specs=[col(i) for i in range(ways)] + [coef, coef],
        out_specs=[pl.BlockSpec((B * nch, sw), lambda j: (0, j)) for _ in range(ways)],
        out_shape=[jax.ShapeDtypeStruct((B * nch, ng // ways * sw), F32) for _ in range(ways)],
        compiler_params=_cparams(("parallel",)),
        name="s5_scan",
    )(*([z] * ways), a1, a2)
    per_way = gpl // ways
    return pl.pallas_call(
        _s5_out_kernel,
        grid=(B, nslab),
        in_specs=[pl.BlockSpec((nch, gpl * gw), lambda b, j: (b, j))]
                 + [pl.BlockSpec((nch, per_way * sw), lambda b, j: (b, j)) for _ in range(ways)]
                 + [ops(gw, gw), ops(gw, gw), ops(sw, gw), ops(sw, gw)],
        out_specs=pl.BlockSpec((1, T, LANES), lambda b, j: (b, 0, j)),
        out_shape=jax.ShapeDtypeStruct((B, T, ng * SSM_GROUP), F32),
        compiler_params=_cparams(("parallel", "parallel")),
        name="s5_out",
    )(ug, *s_ways, m_hi, m_lo, o_hi, o_lo)


def _cmul(ar, ai, br, bi):
    return ar * br - ai * bi, ar * bi + ai * br


def _s5_operators(a_re, a_im, log_dt, b_re, b_im, c_re, c_im):
    hp = lax.Precision.HIGHEST
    L = S5_STEP
    dt = jnp.exp(log_dt)[:, None]
    mag = jnp.exp(a_re * dt)
    lr, li = mag * jnp.cos(a_im * dt), mag * jnp.sin(a_im * dt)
    den = a_re * a_re + a_im * a_im
    fr, fi = _cmul(lr - 1.0, li, a_re / den, -a_im / den)
    bbr, bbi = _cmul(fr[..., None], fi[..., None], b_re, b_im)
    pr, pi = [jnp.ones_like(lr)], [jnp.zeros_like(li)]
    for _ in range(L):
        nr, ni = _cmul(pr[-1], pi[-1], lr, li)
        pr.append(nr)
        pi.append(ni)
    pr, pi = jnp.stack(pr), jnp.stack(pi)
    cpr, cpi = _cmul(c_re[None], c_im[None], pr[:L, :, None, :], pi[:L, :, None, :])
    kd = (jnp.einsum('dgcp,gpe->dgce', cpr, bbr, precision=hp)
          - jnp.einsum('dgcp,gpe->dgce', cpi, bbi, precision=hp))
    lag = jnp.arange(L)[None, :] - jnp.arange(L)[:, None]
    km = jnp.where((lag >= 0)[:, :, None, None, None], kd[jnp.clip(lag, 0, L - 1)], 0.0)
    ng, cg = b_re.shape[0], b_re.shape[2]
    m_op = km.transpose(2, 0, 4, 1, 3).reshape(ng, L * cg, L * cg)
    rr, ri = _cmul(pr[L - 1 - jnp.arange(L)][..., None], pi[L - 1 - jnp.arange(L)][..., None],
                   bbr[None], bbi[None])
    r_op = jnp.concatenate([rr, ri], axis=2).transpose(1, 0, 3, 2).reshape(ng, L * cg, 2 * SSM_STATE)
    orr, oii = _cmul(c_re[None], c_im[None], pr[1:, :, None, :], pi[1:, :, None, :])
    o_op = jnp.concatenate([orr, -oii], axis=3).transpose(1, 3, 0, 2).reshape(ng, 2 * SSM_STATE, L * cg)
    a1 = jnp.concatenate([pr[L], pr[L]], axis=1).reshape(1, -1)
    a2 = jnp.concatenate([-pi[L], pi[L]], axis=1).reshape(1, -1)
    return m_op, r_op, o_op, a1, a2


def _final_kernel(attn_ref, y_ref, u_ref, szs_ref, x_ref, mod_ref, d_ref, gw_ref, gb_ref, wo_ref, o_ref):
    yv = y_ref[0] + d_ref[...] * u_ref[0].astype(F32)
    yg = jax.nn.gelu(yv)
    gl = jax.nn.sigmoid(_dot(yg.astype(BF16), gw_ref[...]) + gb_ref[...])
    ssm = (yg * gl) * szs_ref[0].astype(F32)
    mix = _dot(attn_ref[0], wo_ref[0:ATTN_WIDTH, :]) + _dot(ssm.astype(BF16), wo_ref[ATTN_WIDTH:D_MODEL, :])
    o_ref[0] = x_ref[0] + mod_ref[0, 2:3, :] * mix


def _final(attn, y, u, szs, x, mod3, d_skip, glu_w, glu_b, w_out):
    B, T, D = x.shape
    tm = ROW_TILE
    tok = lambda w: pl.BlockSpec((1, tm, w), lambda b, i: (b, i, 0))
    const = lambda shape: pl.BlockSpec(shape, lambda b, i: tuple(0 for _ in shape))
    return pl.pallas_call(
        _final_kernel,
        grid=(B, T // tm),
        in_specs=[tok(ATTN_WIDTH), tok(SSM_WIDTH), tok(SSM_WIDTH), tok(SSM_WIDTH), tok(D),
                  pl.BlockSpec((1, 3, D), lambda b, i: (b, 0, 0)),
                  const((1, SSM_WIDTH)), const((SSM_WIDTH, SSM_WIDTH)), const((1, SSM_WIDTH)), const((D, D))],
        out_specs=tok(D),
        out_shape=jax.ShapeDtypeStruct((B, T, D), F32),
        compiler_params=_cparams(("parallel", "parallel")),
        name="final",
    )(attn, y, u, szs, x, mod3, d_skip, glu_w, glu_b, w_out)


def _rope_tables(T):
    inv = 1.0 / (ROPE_THETA ** (jnp.arange(0, HEAD_DIM, 2, dtype=F32) / HEAD_DIM))
    ang = jnp.arange(T, dtype=F32)[:, None] * inv[None, :]
    cos, sin = jnp.cos(ang), jnp.sin(ang)
    reps = LANES // HEAD_DIM
    return jnp.tile(jnp.concatenate([cos, cos], axis=1), (1, reps)), jnp.tile(jnp.concatenate([-sin, sin], axis=1), (1, reps))


def _importance_weights(nc):
    r = SEL_BLOCK // CMP_STRIDE
    ov = CMP_LEN // CMP_STRIDE
    w = np.zeros((MAX_SEL_BLOCKS, nc), np.float32)
    for j in range(MAX_SEL_BLOCKS):
        for m in range(r):
            for n in range(ov):
                i = r * j + m - n
                if 0 <= i < nc:
                    w[j, i] += 1.0
    return jnp.asarray(w, BF16)


def _layer(x, c, w_ada, b_ada, norm_g, w_in, q_norm_g, k_cmp_norm_g, k_slc_norm_g, k_win_norm_g,
           cmp_pos_k, cmp_pos_v, cmp_w1_k, cmp_w2_k, cmp_w1_v, cmp_w2_v,
           ssm_a_re, ssm_a_im, ssm_log_dt, ssm_b_re, ssm_b_im, ssm_c_re, ssm_c_im, ssm_d,
           glu_w, glu_b, w_out):
    B, T, D = x.shape
    G = N_KV_GROUPS
    assert D == D_MODEL and T % K_TILE == 0 and T >= WINDOW + Q_TILE and T % ATTN_Q_TILE == 0 and T // SEL_BLOCK <= MAX_SEL_BLOCKS
    assert T // SEL_BLOCK >= SEL_TOPN

    mod3 = _adaln(c, w_ada, b_ada).reshape(B, 3, D)

    o_gbr = ATTN_WIDTH + 6 * KV_WIDTH + ATTN_WIDTH
    o_u = o_gbr + N_BRANCH * N_HEADS
    npg = N_BRANCH * HEADS_PER_GROUP
    gate_cols = [jnp.pad(w_in[:, o_gbr + g * npg:o_gbr + (g + 1) * npg], ((0, 0), (0, LANES - npg))) for g in range(G)]
    w_pad = jnp.concatenate([w_in[:, :o_gbr], w_in[:, o_u:]] + gate_cols, axis=1).astype(BF16)
    assert w_pad.shape[1] == IN_PAD

    cos_t, sin_t = _rope_tables(T)
    hh = np.arange(LANES) // HEAD_DIM
    ones2 = jnp.asarray(hh[:, None] == hh[None, :], BF16)
    tile2 = lambda g: jnp.tile(g, LANES // HEAD_DIM).reshape(1, LANES)
    (q_n, q_r, kcr, vcr, ks_aug, vs_aug, kw, vw_aug, sza, u, szs, gates) = _inproj(
        x, mod3, norm_g.reshape(1, D), w_pad, cos_t, sin_t, ones2,
        tile2(q_norm_g), tile2(k_slc_norm_g), tile2(k_win_norm_g))

    ns = T // CMP_STRIDE
    half = CMP_STRIDE * HEAD_DIM

    def seg_weight(w1_half):
        w4 = w1_half.reshape(CMP_STRIDE, 1, HEAD_DIM, 1, CMP_HIDDEN)
        eye = jnp.eye(G, dtype=F32).reshape(1, G, 1, G, 1)
        return (w4 * eye).reshape(CMP_STRIDE * G * HEAD_DIM, G * CMP_HIDDEN).astype(BF16)

    def out_weight(w2):
        eye = jnp.eye(G, dtype=F32).reshape(G, 1, G, 1)
        return (w2.reshape(1, CMP_HIDDEN, 1, HEAD_DIM) * eye).reshape(G * CMP_HIDDEN, G * HEAD_DIM).astype(BF16)

    pos_rows = lambda p: jnp.broadcast_to(p.reshape(1, CMP_LEN * HEAD_DIM), (8, CMP_LEN * HEAD_DIM))
    kc, vc = _compress(
        kcr.reshape(B, ns, CMP_STRIDE * KV_WIDTH), vcr.reshape(B, ns, CMP_STRIDE * KV_WIDTH),
        seg_weight(cmp_w1_k[:half]), seg_weight(cmp_w1_k[half:]),
        seg_weight(cmp_w1_v[:half]), seg_weight(cmp_w1_v[half:]),
        cmp_w1_k, cmp_w1_v, pos_rows(cmp_pos_k), pos_rows(cmp_pos_v),
        out_weight(cmp_w2_k), out_weight(cmp_w2_v), tile2(k_cmp_norm_g), ones2)

    o_cw, bias = _branches(q_n, q_r, kc, vc, _importance_weights(ns), kw, vw_aug, gates)
    attn = _attn(q_r, bias, ks_aug, vs_aug, o_cw, gates, sza)

    m_op, r_op, o_op, a1, a2 = _s5_operators(ssm_a_re, ssm_a_im, ssm_log_dt, ssm_b_re, ssm_b_im, ssm_c_re, ssm_c_im)
    y = _s5(u, *_split_bf16(r_op), *_split_bf16(m_op), *_split_bf16(o_op), a1, a2)

    return _final(attn, y, u, szs, x, mod3, ssm_d.reshape(1, SSM_WIDTH), glu_w.astype(BF16),
                  glu_b.reshape(1, SSM_WIDTH), w_out.astype(BF16))


def kernel(x, c, w_ada, b_ada, norm_g, w_in, q_norm_g, k_cmp_norm_g, k_slc_norm_g, k_win_norm_g, cmp_pos_k, cmp_pos_v, cmp_w1_k, cmp_w2_k, cmp_w1_v, cmp_w2_v, ssm_a_re, ssm_a_im, ssm_log_dt, ssm_b_re, ssm_b_im, ssm_c_re, ssm_c_im, ssm_d, glu_w, glu_b, w_out):
    params = (w_ada, b_ada, norm_g, w_in, q_norm_g, k_cmp_norm_g, k_slc_norm_g, k_win_norm_g, cmp_pos_k, cmp_pos_v,
              cmp_w1_k, cmp_w2_k, cmp_w1_v, cmp_w2_v, ssm_a_re, ssm_a_im, ssm_log_dt, ssm_b_re, ssm_b_im,
              ssm_c_re, ssm_c_im, ssm_d, glu_w, glu_b, w_out)
    for l in range(w_ada.shape[0]):
        x = _layer(x, c, *(p[l] for p in params))
    return x
```
